```python
import jax, jax.numpy as jnp
from jax import lax
import numpy as np

D_MODEL = 1024
BATCH = 4
SEQ = 4096
DEPTH = 1

MIX_WIDTH = D_MODEL
HG_WIDTH = MIX_WIDTH // 2
HG_HEADS = 4
HG_DK = HG_WIDTH // HG_HEADS
GLA_WIDTH = MIX_WIDTH - HG_WIDTH
GLA_HEADS = 4
GLA_DV = GLA_WIDTH // GLA_HEADS
GLA_DK = GLA_DV // 2
GLA_KEY_WIDTH = GLA_HEADS * GLA_DK
GLA_GATE_RANK = 16
GLA_GATE_TAU = 16.0
CHUNK = 16
D_FF = 2816
CONV_WIDTH = 3
EPS = 1e-6

IN_SIZES = [HG_WIDTH, HG_WIDTH, HG_WIDTH, HG_WIDTH,
            GLA_KEY_WIDTH, GLA_KEY_WIDTH, GLA_WIDTH, GLA_WIDTH,
            GLA_GATE_RANK]
IN_TOTAL = int(sum(IN_SIZES))
IN_SPLITS = [int(s) for s in np.cumsum(IN_SIZES)[:-1]]

kernel_name = "hgrn2_gla_parallel_heads_convffn"


def rmsnorm(x, w):
    xf = x.astype(jnp.float32)
    y = xf * lax.rsqrt(jnp.mean(xf * xf, axis=-1, keepdims=True) + EPS)
    return (y * w.astype(jnp.float32)).astype(x.dtype)


def split_heads(t, n_heads):
    b, t_len, hd = t.shape
    return t.reshape(b, t_len, n_heads, hd // n_heads).transpose(0, 2, 1, 3)


def head_rmsnorm(o, w):
    b, h, t_len, dv = o.shape
    of = o.astype(jnp.float32)
    of = of * lax.rsqrt(jnp.mean(of * of, axis=-1, keepdims=True) + EPS)
    of = of.transpose(0, 2, 1, 3).reshape(b, t_len, h * dv)
    return of * w.astype(jnp.float32)


def chunked_gated_linear_attention(q, k, v, log_f):
    b_sz, h, t_len, dk = q.shape
    dv = v.shape[-1]
    n_chunks = t_len // CHUNK
    q = q.astype(jnp.float32).reshape(b_sz, h, n_chunks, CHUNK, dk)
    k = k.astype(jnp.float32).reshape(b_sz, h, n_chunks, CHUNK, dk)
    v = v.astype(jnp.float32).reshape(b_sz, h, n_chunks, CHUNK, dv)
    cum = jnp.cumsum(log_f.astype(jnp.float32).reshape(b_sz, h, n_chunks, CHUNK, dk), axis=3)
    causal = jnp.tril(jnp.ones((CHUNK, CHUNK), dtype=bool))
    diff = cum[:, :, :, :, None, :] - cum[:, :, :, None, :, :]
    decay = jnp.exp(jnp.where(causal[:, :, None], diff, -jnp.inf))
    scores = jnp.einsum('bhnid,bhnjd,bhnijd->bhnij', q, k, decay)
    o_intra = jnp.einsum('bhnij,bhnje->bhnie', scores, v)
    cum_last = cum[:, :, :, -1:, :]
    q_dec = q * jnp.exp(cum)
    k_dec = k * jnp.exp(cum_last - cum)
    chunk_decay = jnp.exp(cum_last[:, :, :, 0, :])

    def step(state, xs):
        qd, kd, vc, cd = xs
        o = jnp.einsum('bhid,bhde->bhie', qd, state)
        state = cd[..., None] * state + jnp.einsum('bhjd,bhje->bhde', kd, vc)
        return state, o

    xs = tuple(jnp.moveaxis(t, 2, 0) for t in (q_dec, k_dec, v, chunk_decay))
    s0 = jnp.zeros((b_sz, h, dk, dv), jnp.float32)
    _, o_inter = lax.scan(step, s0, xs)
    o_inter = jnp.moveaxis(o_inter, 0, 2)
    return (o_intra + o_inter).reshape(b_sz, h, t_len, dv)


def hybrid_mixer(xn, w_in, lower_bound, w_gla_gate_up, b_gla_gate, hg_norm_w, gla_norm_w, w_out):
    proj = xn @ w_in
    hq, hf, hi, hg, gq, gk, gv, gg, gdown = jnp.split(proj, IN_SPLITS, axis=-1)
    lb = lower_bound.astype(jnp.float32)
    f = lb + (1.0 - lb) * jax.nn.sigmoid(hf.astype(jnp.float32))
    o_h = chunked_gated_linear_attention(
        split_heads(hq, HG_HEADS), split_heads(1.0 - f, HG_HEADS),
        split_heads(hi, HG_HEADS), split_heads(jnp.log(f), HG_HEADS))
    o_h = head_rmsnorm(o_h, hg_norm_w) * jax.nn.silu(hg.astype(jnp.float32))
    gate_logit = (gdown @ w_gla_gate_up + b_gla_gate).astype(jnp.float32)
    log_alpha = jax.nn.log_sigmoid(gate_logit) / GLA_GATE_TAU
    o_g = chunked_gated_linear_attention(
        split_heads(gq, GLA_HEADS) * (GLA_DK ** -0.5), split_heads(gk, GLA_HEADS),
        split_heads(gv, GLA_HEADS), split_heads(log_alpha, GLA_HEADS))
    o_g = head_rmsnorm(o_g, gla_norm_w) * jax.nn.silu(gg.astype(jnp.float32))
    o = jnp.concatenate([o_h, o_g], axis=-1).astype(xn.dtype)
    return o @ w_out


def causal_depthwise_conv(u, w, b):
    t_len = u.shape[1]
    up = jnp.pad(u, ((0, 0), (CONV_WIDTH - 1, 0), (0, 0)))
    out = b
    for j in range(CONV_WIDTH):
        out = out + up[:, j:j + t_len] * w[j]
    return out


def conv_gated_mlp(xn, w_up, conv_w, conv_b, w_down):
    up = xn @ w_up
    a, gate_v = jnp.split(up, 2, axis=-1)
    a = causal_depthwise_conv(a, conv_w, conv_b)
    return (jax.nn.silu(a) * gate_v) @ w_down


def setup_inputs(seed: int = 0) -> dict:
    key = jax.random.key(seed)
    ks = jax.random.split(key, 16)
    f32 = jnp.float32
    nrm = lambda k, shape, scale: jax.random.normal(k, shape, f32) * scale
    return {
        "x": nrm(ks[0], (BATCH, SEQ, D_MODEL), 1.0),
        "attn_norm_w": 1.0 + nrm(ks[1], (DEPTH, D_MODEL), 0.01),
        "w_in": nrm(ks[2], (DEPTH, D_MODEL, IN_TOTAL), D_MODEL ** -0.5),
        "hgrn_lower_bounds": nrm(ks[3], (DEPTH + 1, HG_WIDTH), 0.5),
        "w_gla_gate_up": nrm(ks[4], (DEPTH, GLA_GATE_RANK, GLA_KEY_WIDTH), GLA_GATE_RANK ** -0.5),
        "b_gla_gate": nrm(ks[5], (DEPTH, GLA_KEY_WIDTH), 0.01),
        "hg_norm_w": 1.0 + nrm(ks[6], (DEPTH, HG_WIDTH), 0.01),
        "gla_norm_w": 1.0 + nrm(ks[7], (DEPTH, GLA_WIDTH), 0.01),
        "w_out": nrm(ks[8], (DEPTH, MIX_WIDTH, D_MODEL), MIX_WIDTH ** -0.5),
        "ffn_norm_w": 1.0 + nrm(ks[9], (DEPTH, D_MODEL), 0.01),
        "w_up": nrm(ks[10], (DEPTH, D_MODEL, 2 * D_FF), D_MODEL ** -0.5),
        "conv_w": nrm(ks[11], (DEPTH, CONV_WIDTH, D_FF), CONV_WIDTH ** -0.5),
        "conv_b": nrm(ks[12], (DEPTH, D_FF), 0.01),
        "w_down": nrm(ks[13], (DEPTH, D_FF, D_MODEL), D_FF ** -0.5),
        "final_norm_w": 1.0 + nrm(ks[14], (D_MODEL,), 0.01),
    }


def reference(x, attn_norm_w, w_in, hgrn_lower_bounds, w_gla_gate_up, b_gla_gate,
              hg_norm_w, gla_norm_w, w_out, ffn_norm_w, w_up, conv_w, conv_b, w_down,
              final_norm_w):
    lb_all = jnp.cumsum(jax.nn.softmax(hgrn_lower_bounds.astype(jnp.float32), axis=0), axis=0)
    h = x
    for l in range(DEPTH):
        xn = rmsnorm(h, attn_norm_w[l])
        h = h + hybrid_mixer(xn, w_in[l], lb_all[l], w_gla_gate_up[l], b_gla_gate[l],
                             hg_norm_w[l], gla_norm_w[l], w_out[l])
        xn = rmsnorm(h, ffn_norm_w[l])
        h = h + conv_gated_mlp(xn, w_up[l], conv_w[l], conv_b[l], w_down[l])
    return rmsnorm(h, final_norm_w)
```

```python
import functools

import numpy as np
import jax
import jax.numpy as jnp
from jax import lax
from jax.experimental import pallas as pl
from jax.experimental.pallas import tpu as pltpu

D_MODEL = 1024
HG_WIDTH = 512
GLA_WIDTH = 512
GLA_DK = 64
GLA_KEY_WIDTH = 256
GLA_GATE_RANK = 16
GLA_GATE_TAU = 16.0
D_FF = 2816
EPS = 1e-6

LANE = 128
HEAD = 128
N_HEADS = 8
MIX = N_HEADS * HEAD
PROJ_W = 4 * HG_WIDTH + 4 * GLA_WIDTH + LANE
CHUNK = 128
TIME_BLOCK = 512
FF_TILE = 256
VMEM_LIMIT = 58 * 1024 * 1024

_HQ, _HF, _HI, _HGT = 0, 512, 1024, 1536
_GQ, _GK, _GV, _GG, _GD = 2048, 2560, 3072, 3584, 4096

_F32 = jnp.float32
_BF16 = jnp.bfloat16


def _level_matrices(c):
    r = np.arange(c)[:, None]
    s = np.arange(c)[None, :]
    blocks = [s <= r, s > r, np.zeros((c, c), bool)]
    masks = [np.eye(c, dtype=bool)]
    m = 1
    while m < c:
        ref = (r // (2 * m)) * (2 * m) + m - 1
        hi = (r // m) % 2 == 1
        blocks.append(np.where(hi, (s > ref) & (s <= r), (s > r) & (s <= ref)))
        masks.append(((r // (2 * m)) == (s // (2 * m))) & hi & ((s // m) % 2 == 0))
        m *= 2
    n = np.stack(blocks).astype(np.float32)
    n2 = np.concatenate([n, n], axis=-1)
    return n2, np.stack(masks).astype(np.float32)


def _sigmoid(x):
    return 1.0 / (1.0 + jnp.exp(-x))


def _split_bf16(x):
    hi = x.astype(_BF16)
    lo = (x - hi.astype(_F32)).astype(_BF16)
    return hi, lo


def _dot(a, b):
    return jnp.dot(a, b, preferred_element_type=_F32)


def _dot_nt(a, b):
    return lax.dot_general(a, b, (((1,), (1,)), ((), ())), preferred_element_type=_F32)


def _mixer_kernel(x_ref, nw_ref, win_ref, lb_ref, wgh_ref, wgl_ref, bg_ref, hnw_ref,
                  wout_ref, n2_ref, mask_ref, o_ref,
                  proj_scr, q_scr, k_scr, v_scr, g2_scr, a_scr, st_scr, o_scr):
    n_levels = mask_ref.shape[0]

    @pl.when(pl.program_id(1) == 0)
    def _():
        st_scr[...] = jnp.zeros_like(st_scr)

    x = x_ref[0]
    xn = x * lax.rsqrt(jnp.mean(x * x, axis=-1, keepdims=True) + EPS) * nw_ref[...]
    xn = xn.astype(_BF16)
    for c0 in range(0, PROJ_W, 512):
        w = min(512, PROJ_W - c0)
        proj_scr[:, c0:c0 + w] = _dot(xn, win_ref[:, c0:c0 + w])

    p = lb_ref[...]
    pe = jnp.exp(p - jnp.max(p, axis=0, keepdims=True))
    lb = pe[0:1, :] / jnp.sum(pe, axis=0, keepdims=True)

    def chunk_body(ci, carry):
        rows = pl.ds(pl.multiple_of(ci * CHUNK, CHUNK), CHUNK)

        f = lb + (1.0 - lb) * _sigmoid(proj_scr[rows, _HF:_HF + 512])
        gd_hi, gd_lo = _split_bf16(proj_scr[rows, _GD:_GD + LANE])
        logit = (_dot(gd_hi, wgh_ref[...]) + _dot(gd_lo, wgh_ref[...])
                 + _dot(gd_hi, wgl_ref[...]) + bg_ref[...])
        log_alpha = (jnp.minimum(logit, 0.0)
                     - jnp.log(1.0 + jnp.exp(-jnp.abs(logit)))) * (1.0 / GLA_GATE_TAU)
        g_hi_h, g_lo_h = _split_bf16(jnp.log(f))
        g_hi_g, g_lo_g = _split_bf16(log_alpha)
        g2_scr[0:CHUNK, 0:512] = g_hi_h
        g2_scr[0:CHUNK, 512:1024] = g_hi_g
        g2_scr[CHUNK:2 * CHUNK, 0:512] = g_lo_h
        g2_scr[CHUNK:2 * CHUNK, 512:1024] = g_lo_g
        q_scr[:, 0:512] = proj_scr[rows, _HQ:_HQ + 512]
        q_scr[:, 512:1024] = proj_scr[rows, _GQ:_GQ + 512] * (GLA_DK ** -0.5)
        k_scr[:, 0:512] = 1.0 - f
        k_scr[:, 512:1024] = proj_scr[rows, _GK:_GK + 512]
        v_scr[:, 0:512] = proj_scr[rows, _HI:_HI + 512].astype(_BF16)
        v_scr[:, 512:1024] = proj_scr[rows, _GV:_GV + 512].astype(_BF16)

        a_scr[...] = jnp.zeros_like(a_scr)

        def level_body(lv, c2):
            e = jnp.exp(_dot(n2_ref[lv + 2], g2_scr[...]))
            qp = (q_scr[...] * e).astype(_BF16)
            kp = (k_scr[...] * e).astype(_BF16)
            m = mask_ref[lv]
            for h in range(N_HEADS):
                hs = slice(h * HEAD, (h + 1) * HEAD)
                a_scr[h] += _dot_nt(qp[:, hs], kp[:, hs]) * m
            return c2

        lax.fori_loop(0, n_levels, level_body, 0)

        e_q = jnp.exp(_dot(n2_ref[0], g2_scr[...]))
        e_k = jnp.exp(_dot(n2_ref[1], g2_scr[...]))
        qd = (q_scr[...] * e_q).astype(_BF16)
        kd = (k_scr[...] * e_k).astype(_BF16)
        chunk_decay = e_q[CHUNK - 1:CHUNK, :]
        for h in range(N_HEADS):
            hs = slice(h * HEAD, (h + 1) * HEAD)
            st = st_scr[h]
            v_h = v_scr[:, hs]
            o_h = _dot(a_scr[h].astype(_BF16), v_h) + _dot_nt(qd[:, hs], st.astype(_BF16))
            vt = v_h.astype(_F32).T.astype(_BF16)
            st_scr[h] = st * chunk_decay[:, hs] + _dot(vt, kd[:, hs])
            o_h = o_h * lax.rsqrt(jnp.mean(o_h * o_h, axis=-1, keepdims=True) + EPS)
            gate_col = (_HGT + h * HEAD) if h < 4 else (_GG + (h - 4) * HEAD)
            gate = proj_scr[rows, gate_col:gate_col + HEAD]
            o_h = o_h * hnw_ref[:, hs] * (gate * _sigmoid(gate))
            o_scr[rows, hs] = o_h.astype(_BF16)
        return carry

    lax.fori_loop(0, TIME_BLOCK // CHUNK, chunk_body, 0)

    o_ref[0] = x + _dot(o_scr[...], wout_ref[...])


def _ffn_kernel(h_ref, nw_ref, wup_ref, cw_ref, cb_ref, wdn_ref, fw_ref, o_ref,
                carry_scr, acc_scr):
    @pl.when(pl.program_id(1) == 0)
    def _():
        carry_scr[...] = jnp.zeros_like(carry_scr)

    h = h_ref[0]
    hn = h * lax.rsqrt(jnp.mean(h * h, axis=-1, keepdims=True) + EPS) * nw_ref[...]
    hn = hn.astype(_BF16)
    tb = h.shape[0]
    row = lax.broadcasted_iota(jnp.int32, (tb, FF_TILE), 0)
    acc_scr[...] = h
    for c0 in range(0, D_FF, FF_TILE):
        cs = slice(c0, c0 + FF_TILE)
        a = _dot(hn, wup_ref[:, cs])
        gate = _dot(hn, wup_ref[:, D_FF + c0:D_FF + c0 + FF_TILE])
        prev = carry_scr[:, cs]
        carry_scr[:, cs] = a[tb - 8:tb, :]
        a1 = jnp.where(row == 0, prev[7:8, :], pltpu.roll(a, 1, 0))
        a2 = jnp.where(row == 0, prev[6:7, :],
                       jnp.where(row == 1, prev[7:8, :], pltpu.roll(a, 2, 0)))
        u = cb_ref[:, cs] + a2 * cw_ref[0:1, cs] + a1 * cw_ref[1:2, cs] + a * cw_ref[2:3, cs]
        u = u * _sigmoid(u) * gate
        acc_scr[...] += _dot(u.astype(_BF16), wdn_ref[cs, :])
    y = acc_scr[...]
    o_ref[0] = y * lax.rsqrt(jnp.mean(y * y, axis=-1, keepdims=True) + EPS) * fw_ref[...]


def _const_spec(shape):
    zeros = (0,) * len(shape)
    return pl.BlockSpec(shape, lambda b, t: zeros, pipeline_mode=pl.Buffered(1))


def _pad_heads(w, n_heads, dk):
    lead = w.shape[:-1]
    w = w.reshape(lead + (n_heads, dk))
    w = jnp.pad(w, [(0, 0)] * len(lead) + [(0, 0), (0, HEAD - dk)])
    return w.reshape(lead + (n_heads * HEAD,))


def kernel(x, attn_norm_w, w_in, hgrn_lower_bounds, w_gla_gate_up, b_gla_gate, hg_norm_w,
           gla_norm_w, w_out, ffn_norm_w, w_up, conv_w, conv_b, w_down, final_norm_w):
    batch, seq, d = x.shape
    assert d == D_MODEL and seq % TIME_BLOCK == 0
    assert w_in.shape[0] == 1, "single-layer problem"

    wi = w_in[0]
    o = 4 * HG_WIDTH
    w_gq = _pad_heads(wi[:, o:o + GLA_KEY_WIDTH], 4, GLA_DK)
    w_gk = _pad_heads(wi[:, o + GLA_KEY_WIDTH:o + 2 * GLA_KEY_WIDTH], 4, GLA_DK)
    o2 = o + 2 * GLA_KEY_WIDTH
    w_gd = jnp.pad(wi[:, o2 + 2 * GLA_WIDTH:], ((0, 0), (0, LANE - GLA_GATE_RANK)))
    win_p = jnp.concatenate([wi[:, :o], w_gq, w_gk, wi[:, o2:o2 + 2 * GLA_WIDTH], w_gd],
                            axis=1).astype(_BF16)
    wgu = jnp.pad(_pad_heads(w_gla_gate_up[0], 4, GLA_DK), ((0, LANE - GLA_GATE_RANK), (0, 0)))
    wgu_hi = wgu.astype(_BF16)
    wgu_lo = (wgu - wgu_hi.astype(_F32)).astype(_BF16)
    bg = _pad_heads(b_gla_gate[0], 4, GLA_DK)[None, :]
    head_nw = jnp.concatenate([hg_norm_w[0], gla_norm_w[0]])[None, :]
    n2_np, mask_np = _level_matrices(CHUNK)
    n2 = jnp.asarray(n2_np, dtype=_BF16)
    masks = jnp.asarray(mask_np)

    params = pltpu.CompilerParams(dimension_semantics=("arbitrary", "arbitrary"),
                                  vmem_limit_bytes=VMEM_LIMIT)
    grid = (batch, seq // TIME_BLOCK)
    act_spec = pl.BlockSpec((1, TIME_BLOCK, D_MODEL), lambda b, t: (b, t, 0))

    h = pl.pallas_call(
        _mixer_kernel,
        grid=grid,
        in_specs=[
            act_spec,
            _const_spec((1, D_MODEL)),
            _const_spec((D_MODEL, PROJ_W)),
            _const_spec(hgrn_lower_bounds.shape),
            _const_spec((LANE, 512)),
            _const_spec((LANE, 512)),
            _const_spec((1, 512)),
            _const_spec((1, MIX)),
            _const_spec((MIX, D_MODEL)),
            _const_spec(n2.shape),
            _const_spec(masks.shape),
        ],
        out_specs=act_spec,
        out_shape=jax.ShapeDtypeStruct(x.shape, _F32),
        scratch_shapes=[
            pltpu.VMEM((TIME_BLOCK, PROJ_W), _F32),
            pltpu.VMEM((CHUNK, MIX), _F32),
            pltpu.VMEM((CHUNK, MIX), _F32),
            pltpu.VMEM((CHUNK, MIX), _BF16),
            pltpu.VMEM((2 * CHUNK, MIX), _BF16),
            pltpu.VMEM((N_HEADS, CHUNK, CHUNK), _F32),
            pltpu.VMEM((N_HEADS, HEAD, HEAD), _F32),
            pltpu.VMEM((TIME_BLOCK, MIX), _BF16),
        ],
        compiler_params=params,
        name="mixer",
    )(x, attn_norm_w, win_p, hgrn_lower_bounds, wgu_hi, wgu_lo, bg, head_nw,
      w_out[0].astype(_BF16), n2, masks)

    out = pl.pallas_call(
        _ffn_kernel,
        grid=grid,
        in_specs=[
            act_spec,
            _const_spec((1, D_MODEL)),
            _const_spec((D_MODEL, 2 * D_FF)),
            _const_spec((3, D_FF)),
            _const_spec((1, D_FF)),
            _const_spec((D_FF, D_MODEL)),
            _const_spec((1, D_MODEL)),
        ],
        out_specs=act_spec,
        out_shape=jax.ShapeDtypeStruct(x.shape, _F32),
        scratch_shapes=[
            pltpu.VMEM((8, D_FF), _F32),
            pltpu.VMEM((TIME_BLOCK, D_MODEL), _F32),
        ],
        compiler_params=params,
        name="ffn",
    )(h, ffn_norm_w, w_up[0].astype(_BF16), conv_w[0], conv_b, w_down[0].astype(_BF16),
      final_norm_w[None, :])
    return out
```

```python
import numpy as np
import jax
import jax.numpy as jnp
from jax import lax
from jax.experimental import pallas as pl
from jax.experimental.pallas import tpu as pltpu

D_MODEL = 1024
HG_WIDTH = 512
GLA_WIDTH = 512
GLA_DK = 64
GLA_KEY_WIDTH = 256
GLA_GATE_RANK = 16
GLA_GATE_TAU = 16.0
D_FF = 2816
EPS = 1e-6
LOG2_E = 1.4426950408889634

LANE = 128
HEAD = 128
N_HEADS = 8
MIX = N_HEADS * HEAD
SLAB = 2 * HEAD
PROJ_W = 4 * HG_WIDTH + 4 * GLA_WIDTH + LANE
CHUNK = 128
TIME_BLOCK = 512
FF_TILE = 256
VMEM_LIMIT = 58 * 1024 * 1024

_HQ, _HF, _HI, _HGT = 0, 512, 1024, 1536
_GQ, _GK, _GV, _GG, _GD = 2048, 2560, 3072, 3584, 4096

_F32 = jnp.float32
_BF16 = jnp.bfloat16


def _level_matrices(c):
    r = np.arange(c)[:, None]
    s = np.arange(c)[None, :]
    blocks = [s <= r, s > r]
    masks = [np.eye(c, dtype=bool)]
    m = 1
    while m < c:
        ref = (r // (2 * m)) * (2 * m) + m - 1
        hi = (r // m) % 2 == 1
        blocks.append(np.where(hi, (s > ref) & (s <= r), (s > r) & (s <= ref)))
        masks.append(((r // (2 * m)) == (s // (2 * m))) & hi & ((s // m) % 2 == 0))
        m *= 2
    n = np.concatenate(blocks, axis=0).astype(np.float32)
    n2 = np.concatenate([n, n], axis=-1)
    return n2, np.stack(masks).astype(np.float32)


def _sigmoid(x):
    return 1.0 / (1.0 + jnp.exp(-x))


def _split_bf16(x):
    hi = x.astype(_BF16)
    lo = (x - hi.astype(_F32)).astype(_BF16)
    return hi, lo


def _dot(a, b):
    return jnp.dot(a, b, preferred_element_type=_F32)


def _dot_nt(a, b):
    return lax.dot_general(a, b, (((1,), (1,)), ((), ())), preferred_element_type=_F32)


def _mixer_kernel(x_ref, nw_ref, win_ref, lb_ref, wgh_ref, wgl_ref, bg_ref, hnw_ref,
                  wout_ref, n2_ref, mask_ref, o_ref,
                  proj_scr, xs0, xs1, xs2, xs3, k_scr, a_scr, st_scr, o_scr):
    n_levels = mask_ref.shape[0]
    x_scrs = (xs0, xs1, xs2, xs3)

    @pl.when(pl.program_id(1) == 0)
    def _():
        st_scr[...] = jnp.zeros_like(st_scr)

    x = x_ref[0]
    xn = x * lax.rsqrt(jnp.mean(x * x, axis=-1, keepdims=True) + EPS) * nw_ref[...]
    xn = xn.astype(_BF16)
    for c0 in range(0, PROJ_W, 512):
        w = min(512, PROJ_W - c0)
        proj_scr[:, c0:c0 + w] = _dot(xn, win_ref[:, c0:c0 + w])

    p = lb_ref[...]
    pe = jnp.exp(p - jnp.max(p, axis=0, keepdims=True))
    lb = pe[0:1, :] / jnp.sum(pe, axis=0, keepdims=True)

    def chunk_body(ci, carry):
        rows = pl.ds(pl.multiple_of(ci * CHUNK, CHUNK), CHUNK)
        n_slabs = MIX // SLAB

        def cols(s):
            if s < 2:
                c0 = s * SLAB
                return _HQ + c0, None, _HI + c0, _HGT + c0
            c0 = (s - 2) * SLAB
            return _GQ + c0, _GK + c0, _GV + c0, _GG + c0

        def load_qk(s):
            q_col, k_col, _, _ = cols(s)
            q = proj_scr[rows, q_col:q_col + SLAB]
            if k_col is None:
                k = k_scr[:, s * SLAB:(s + 1) * SLAB]
            else:
                k = proj_scr[rows, k_col:k_col + SLAB]
            return q, k

        gd_hi, gd_lo = _split_bf16(proj_scr[rows, _GD:_GD + LANE])
        for s in range(n_slabs):
            if s < 2:
                c0 = s * SLAB
                lb_s = lb[:, c0:c0 + SLAB]
                f = lb_s + (1.0 - lb_s) * _sigmoid(proj_scr[rows, _HF + c0:_HF + c0 + SLAB])
                g = jnp.log2(f)
                k_scr[:, c0:c0 + SLAB] = 1.0 - f
            else:
                cs = slice((s - 2) * SLAB, (s - 1) * SLAB)
                logit = (_dot(gd_hi, wgh_ref[:, cs]) + _dot(gd_lo, wgh_ref[:, cs])
                         + _dot(gd_hi, wgl_ref[:, cs]) + bg_ref[:, cs])
                g = (jnp.minimum(logit, 0.0) * LOG2_E
                     - jnp.log2(1.0 + jnp.exp(-jnp.abs(logit)))) * (1.0 / GLA_GATE_TAU)
            g_hi, g_lo = _split_bf16(g)
            x_scrs[s][...] = _dot(n2_ref[...], jnp.concatenate([g_hi, g_lo], axis=0))

        for s in range(n_slabs):
            q, k = load_qk(s)
            acc = [None, None]
            for lv in range(n_levels):
                if lv == 0:
                    qp, kp = q.astype(_BF16), k.astype(_BF16)
                else:
                    e = jnp.exp2(x_scrs[s][(lv + 1) * CHUNK:(lv + 2) * CHUNK, :])
                    qp, kp = (q * e).astype(_BF16), (k * e).astype(_BF16)
                m = mask_ref[lv]
                for j in range(2):
                    js = slice(j * HEAD, (j + 1) * HEAD)
                    a = _dot_nt(qp[:, js], kp[:, js]) * m
                    acc[j] = a if acc[j] is None else acc[j] + a
            for j in range(2):
                a_scr[2 * s + j] = acc[j].astype(_BF16)

        for s in range(n_slabs):
            q, k = load_qk(s)
            _, _, v_col, gate_col = cols(s)
            e_q = jnp.exp2(x_scrs[s][0:CHUNK, :])
            e_k = jnp.exp2(x_scrs[s][CHUNK:2 * CHUNK, :])
            qd = (q * e_q).astype(_BF16)
            kd = (k * e_k).astype(_BF16)
            chunk_decay = e_q[CHUNK - 1:CHUNK, :]
            for j in range(2):
                js = slice(j * HEAD, (j + 1) * HEAD)
                h = 2 * s + j
                hs = slice(h * HEAD, (h + 1) * HEAD)
                st = st_scr[h]
                v_f = proj_scr[rows, v_col + j * HEAD:v_col + (j + 1) * HEAD]
                v_h = v_f.astype(_BF16)
                o_h = _dot(a_scr[h], v_h) + _dot_nt(qd[:, js], st.astype(_BF16))
                st_scr[h] = st * chunk_decay[:, js] + _dot(v_f.T.astype(_BF16), kd[:, js])
                o_h = o_h * lax.rsqrt(jnp.mean(o_h * o_h, axis=-1, keepdims=True) + EPS)
                gate = proj_scr[rows, gate_col + j * HEAD:gate_col + (j + 1) * HEAD]
                o_h = o_h * hnw_ref[:, hs] * (gate * _sigmoid(gate))
                o_scr[rows, hs] = o_h.astype(_BF16)
        return carry

    lax.fori_loop(0, TIME_BLOCK // CHUNK, chunk_body, 0, unroll=2)

    o_ref[0] = x + _dot(o_scr[...], wout_ref[...])


def _ffn_kernel(h_ref, nw_ref, wup_ref, cw_ref, cb_ref, wdn_ref, fw_ref, o_ref,
                carry_scr, acc_scr):
    @pl.when(pl.program_id(1) == 0)
    def _():
        carry_scr[...] = jnp.zeros_like(carry_scr)

    h = h_ref[0]
    hn = h * lax.rsqrt(jnp.mean(h * h, axis=-1, keepdims=True) + EPS) * nw_ref[...]
    hn = hn.astype(_BF16)
    tb = h.shape[0]
    row = lax.broadcasted_iota(jnp.int32, (tb, FF_TILE), 0)
    acc_scr[...] = h
    for c0 in range(0, D_FF, FF_TILE):
        cs = slice(c0, c0 + FF_TILE)
        a = _dot(hn, wup_ref[:, cs])
        gate = _dot(hn, wup_ref[:, D_FF + c0:D_FF + c0 + FF_TILE])
        prev = carry_scr[:, cs]
        carry_scr[:, cs] = a[tb - 8:tb, :]
        a1 = jnp.where(row == 0, prev[7:8, :], pltpu.roll(a, 1, 0))
        a2 = jnp.where(row == 0, prev[6:7, :],
                       jnp.where(row == 1, prev[7:8, :], pltpu.roll(a, 2, 0)))
        u = cb_ref[:, cs] + a2 * cw_ref[0:1, cs] + a1 * cw_ref[1:2, cs] + a * cw_ref[2:3, cs]
        u = u * _sigmoid(u) * gate
        acc_scr[...] += _dot(u.astype(_BF16), wdn_ref[cs, :])
    y = acc_scr[...]
    o_ref[0] = y * lax.rsqrt(jnp.mean(y * y, axis=-1, keepdims=True) + EPS) * fw_ref[...]


def _const_spec(shape):
    zeros = (0,) * len(shape)
    return pl.BlockSpec(shape, lambda b, t: zeros, pipeline_mode=pl.Buffered(1))


def _pad_heads(w, n_heads, dk):
    lead = w.shape[:-1]
    w = w.reshape(lead + (n_heads, dk))
    w = jnp.pad(w, [(0, 0)] * len(lead) + [(0, 0), (0, HEAD - dk)])
    return w.reshape(lead + (n_heads * HEAD,))


def kernel(x, attn_norm_w, w_in, hgrn_lower_bounds, w_gla_gate_up, b_gla_gate, hg_norm_w,
           gla_norm_w, w_out, ffn_norm_w, w_up, conv_w, conv_b, w_down, final_norm_w):
    batch, seq, d = x.shape
    assert d == D_MODEL and seq % TIME_BLOCK == 0
    assert w_in.shape[0] == 1, "single-layer problem"

    wi = w_in[0]
    o = 4 * HG_WIDTH
    w_gq = _pad_heads(wi[:, o:o + GLA_KEY_WIDTH] * (GLA_DK ** -0.5), 4, GLA_DK)
    w_gk = _pad_heads(wi[:, o + GLA_KEY_WIDTH:o + 2 * GLA_KEY_WIDTH], 4, GLA_DK)
    o2 = o + 2 * GLA_KEY_WIDTH
    w_gd = jnp.pad(wi[:, o2 + 2 * GLA_WIDTH:], ((0, 0), (0, LANE - GLA_GATE_RANK)))
    win_p = jnp.concatenate([wi[:, :o], w_gq, w_gk, wi[:, o2:o2 + 2 * GLA_WIDTH], w_gd],
                            axis=1).astype(_BF16)
    wgu = jnp.pad(_pad_heads(w_gla_gate_up[0], 4, GLA_DK), ((0, LANE - GLA_GATE_RANK), (0, 0)))
    wgu_hi = wgu.astype(_BF16)
    wgu_lo = (wgu - wgu_hi.astype(_F32)).astype(_BF16)
    bg = _pad_heads(b_gla_gate[0], 4, GLA_DK)[None, :]
    head_nw = jnp.concatenate([hg_norm_w[0], gla_norm_w[0]])[None, :]
    n2_np, mask_np = _level_matrices(CHUNK)
    n2 = jnp.asarray(n2_np, dtype=_BF16)
    masks = jnp.asarray(mask_np)

    params = pltpu.CompilerParams(dimension_semantics=("arbitrary", "arbitrary"),
                                  vmem_limit_bytes=VMEM_LIMIT)
    grid = (batch, seq // TIME_BLOCK)
    act_spec = pl.BlockSpec((1, TIME_BLOCK, D_MODEL), lambda b, t: (b, t, 0))

    h = pl.pallas_call(
        _mixer_kernel,
        grid=grid,
        in_specs=[
            act_spec,
            _const_spec((1, D_MODEL)),
            _const_spec((D_MODEL, PROJ_W)),
            _const_spec(hgrn_lower_bounds.shape),
            _const_spec((LANE, 512)),
            _const_spec((LANE, 512)),
            _const_spec((1, 512)),
            _const_spec((1, MIX)),
            _const_spec((MIX, D_MODEL)),
            _const_spec(n2.shape),
            _const_spec(masks.shape),
        ],
        out_specs=act_spec,
        out_shape=jax.ShapeDtypeStruct(x.shape, _F32),
        scratch_shapes=[
            pltpu.VMEM((TIME_BLOCK, PROJ_W), _F32),
            *[pltpu.VMEM((n2.shape[0], SLAB), _F32)
              for _ in range(MIX // SLAB)],
            pltpu.VMEM((CHUNK, HG_WIDTH), _F32),
            pltpu.VMEM((N_HEADS, CHUNK, CHUNK), _BF16),
            pltpu.VMEM((N_HEADS, HEAD, HEAD), _F32),
            pltpu.VMEM((TIME_BLOCK, MIX), _BF16),
        ],
        compiler_params=params,
        name="mixer",
    )(x, attn_norm_w, win_p, hgrn_lower_bounds, wgu_hi, wgu_lo, bg, head_nw,
      w_out[0].astype(_BF16), n2, masks)

    out = pl.pallas_call(
        _ffn_kernel,
        grid=grid,
        in_specs=[
            act_spec,
            _const_spec((1, D_MODEL)),
            _const_spec((D_MODEL, 2 * D_FF)),
            _const_spec((3, D_FF)),
            _const_spec((1, D_FF)),
            _const_spec((D_FF, D_MODEL)),
            _const_spec((1, D_MODEL)),
        ],
        out_specs=act_spec,
        out_shape=jax.ShapeDtypeStruct(x.shape, _F32),
        scratch_shapes=[
            pltpu.VMEM((8, D_FF), _F32),
            pltpu.VMEM((TIME_BLOCK, D_MODEL), _F32),
        ],
        compiler_params=params,
        name="ffn",
    )(h, ffn_norm_w, w_up[0].astype(_BF16), conv_w[0], conv_b, w_down[0].astype(_BF16),
      final_norm_w[None, :])
    return out
```

```python
import numpy as np
import jax
import jax.numpy as jnp
from jax import lax
from jax.experimental import pallas as pl
from jax.experimental.pallas import tpu as pltpu

D_MODEL = 1024
HG_WIDTH = 512
GLA_WIDTH = 512
GLA_DK = 64
GLA_KEY_WIDTH = 256
GLA_GATE_RANK = 16
GLA_GATE_TAU = 16.0
D_FF = 2816
EPS = 1e-6
LOG2_E = 1.4426950408889634

LANE = 128
HEAD = 128
N_HEADS = 8
MIX = N_HEADS * HEAD
SLAB = 2 * HEAD
PROJ_W = 4 * HG_WIDTH + 4 * GLA_WIDTH + LANE
CHUNK = 128
TIME_BLOCK = 512
FF_TILE = 256
HALO = 8
VMEM_LIMIT = 58 * 1024 * 1024

_HQ, _HF, _HI, _HGT = 0, 512, 1024, 1536
_GQ, _GK, _GV, _GG, _GD = 2048, 2560, 3072, 3584, 4096

_F32 = jnp.float32
_BF16 = jnp.bfloat16


def _level_matrices(c):
    r = np.arange(c)[:, None]
    s = np.arange(c)[None, :]
    blocks = [s <= r, s > r]
    masks = [np.eye(c, dtype=bool)]
    m = 1
    while m < c:
        ref = (r // (2 * m)) * (2 * m) + m - 1
        hi = (r // m) % 2 == 1
        blocks.append(np.where(hi, (s > ref) & (s <= r), (s > r) & (s <= ref)))
        masks.append(((r // (2 * m)) == (s // (2 * m))) & hi & ((s // m) % 2 == 0))
        m *= 2
    n = np.concatenate(blocks, axis=0).astype(np.float32)
    n2 = np.concatenate([n, n], axis=-1)
    return n2, np.stack(masks).astype(np.float32)


def _sigmoid(x):
    return 1.0 / (1.0 + jnp.exp(-x))


def _split_bf16(x):
    hi = x.astype(_BF16)
    lo = (x - hi.astype(_F32)).astype(_BF16)
    return hi, lo


def _dot(a, b):
    return jnp.dot(a, b, preferred_element_type=_F32)


def _dot_nt(a, b):
    return lax.dot_general(a, b, (((1,), (1,)), ((), ())), preferred_element_type=_F32)


def _mixer_kernel(x_ref, nw_ref, win_ref, lb_ref, wgh_ref, wgl_ref, bg_ref, hnw_ref,
                  wout_ref, n2_ref, mask_ref, o_ref,
                  proj_scr, xs0, xs1, xs2, xs3, k_scr, a_scr, st_scr, o_scr):
    n_levels = mask_ref.shape[0]
    x_scrs = (xs0, xs1, xs2, xs3)

    @pl.when(pl.program_id(1) == 0)
    def _():
        st_scr[...] = jnp.zeros_like(st_scr)

    x = x_ref[0]
    xn = x * lax.rsqrt(jnp.mean(x * x, axis=-1, keepdims=True) + EPS) * nw_ref[...]
    xn = xn.astype(_BF16)
    for c0 in range(0, PROJ_W, 512):
        w = min(512, PROJ_W - c0)
        proj_scr[:, c0:c0 + w] = _dot(xn, win_ref[:, c0:c0 + w])

    p = lb_ref[...]
    pe = jnp.exp(p - jnp.max(p, axis=0, keepdims=True))
    lb = pe[0:1, :] / jnp.sum(pe, axis=0, keepdims=True)

    def chunk_body(ci, carry):
        rows = pl.ds(pl.multiple_of(ci * CHUNK, CHUNK), CHUNK)
        n_slabs = MIX // SLAB

        def cols(s):
            if s < 2:
                c0 = s * SLAB
                return _HQ + c0, None, _HI + c0, _HGT + c0
            c0 = (s - 2) * SLAB
            return _GQ + c0, _GK + c0, _GV + c0, _GG + c0

        def load_qk(s):
            q_col, k_col, _, _ = cols(s)
            q = proj_scr[rows, q_col:q_col + SLAB]
            if k_col is None:
                k = k_scr[:, s * SLAB:(s + 1) * SLAB]
            else:
                k = proj_scr[rows, k_col:k_col + SLAB]
            return q, k

        gd_hi, gd_lo = _split_bf16(proj_scr[rows, _GD:_GD + LANE])
        for s in range(n_slabs):
            if s < 2:
                c0 = s * SLAB
                lb_s = lb[:, c0:c0 + SLAB]
                f = lb_s + (1.0 - lb_s) * _sigmoid(proj_scr[rows, _HF + c0:_HF + c0 + SLAB])
                g = jnp.log2(f)
                k_scr[:, c0:c0 + SLAB] = 1.0 - f
            else:
                cs = slice((s - 2) * SLAB, (s - 1) * SLAB)
                logit = (_dot(gd_hi, wgh_ref[:, cs]) + _dot(gd_lo, wgh_ref[:, cs])
                         + _dot(gd_hi, wgl_ref[:, cs]) + bg_ref[:, cs])
                g = (jnp.minimum(logit, 0.0) * LOG2_E
                     - jnp.log2(1.0 + jnp.exp(-jnp.abs(logit)))) * (1.0 / GLA_GATE_TAU)
            g_hi, g_lo = _split_bf16(g)
            x_scrs[s][...] = _dot(n2_ref[...], jnp.concatenate([g_hi, g_lo], axis=0))

        for s in range(n_slabs):
            q, k = load_qk(s)
            acc = [None, None]
            for lv in range(n_levels):
                if lv == 0:
                    qp, kp = q.astype(_BF16), k.astype(_BF16)
                else:
                    e = jnp.exp2(x_scrs[s][(lv + 1) * CHUNK:(lv + 2) * CHUNK, :])
                    qp, kp = (q * e).astype(_BF16), (k * e).astype(_BF16)
                m = mask_ref[lv]
                for j in range(2):
                    js = slice(j * HEAD, (j + 1) * HEAD)
                    a = _dot_nt(qp[:, js], kp[:, js]) * m
                    acc[j] = a if acc[j] is None else acc[j] + a
            for j in range(2):
                a_scr[2 * s + j] = acc[j].astype(_BF16)

        for s in range(n_slabs):
            q, k = load_qk(s)
            _, _, v_col, gate_col = cols(s)
            e_q = jnp.exp2(x_scrs[s][0:CHUNK, :])
            e_k = jnp.exp2(x_scrs[s][CHUNK:2 * CHUNK, :])
            qd = (q * e_q).astype(_BF16)
            kd = (k * e_k).astype(_BF16)
            chunk_decay = e_q[CHUNK - 1:CHUNK, :]
            for j in range(2):
                js = slice(j * HEAD, (j + 1) * HEAD)
                h = 2 * s + j
                hs = slice(h * HEAD, (h + 1) * HEAD)
                st = st_scr[h]
                v_f = proj_scr[rows, v_col + j * HEAD:v_col + (j + 1) * HEAD]
                v_h = v_f.astype(_BF16)
                o_h = _dot(a_scr[h], v_h) + _dot_nt(qd[:, js], st.astype(_BF16))
                st_scr[h] = st * chunk_decay[:, js] + _dot(v_f.T.astype(_BF16), kd[:, js])
                o_h = o_h * lax.rsqrt(jnp.mean(o_h * o_h, axis=-1, keepdims=True) + EPS)
                gate = proj_scr[rows, gate_col + j * HEAD:gate_col + (j + 1) * HEAD]
                o_h = o_h * hnw_ref[:, hs] * (gate * _sigmoid(gate))
                o_scr[rows, hs] = o_h.astype(_BF16)
        return carry

    lax.fori_loop(0, TIME_BLOCK // CHUNK, chunk_body, 0, unroll=2)

    o_ref[0] = x + _dot(o_scr[...], wout_ref[...])


def _ffn_kernel(h_ref, nw_ref, wup_ref, cw_ref, cb_ref, wdn_ref, fw_ref, o_ref,
                a_ext, u_scr):
    tb = h_ref.shape[1]

    @pl.when(pl.program_id(1) == 0)
    def _():
        a_ext[0:HALO, :] = jnp.zeros((HALO, D_FF), _F32)

    @pl.when(pl.program_id(1) > 0)
    def _():
        a_ext[0:HALO, :] = a_ext[tb:tb + HALO, :]

    h = h_ref[0]
    hn = h * lax.rsqrt(jnp.mean(h * h, axis=-1, keepdims=True) + EPS) * nw_ref[...]
    hn = hn.astype(_BF16)
    for c0 in range(0, D_FF, FF_TILE):
        cs = slice(c0, c0 + FF_TILE)
        a_ext[HALO:HALO + tb, cs] = _dot(hn, wup_ref[:, cs])
        gate = _dot(hn, wup_ref[:, D_FF + c0:D_FF + c0 + FF_TILE])
        u = (cb_ref[:, cs]
             + a_ext[HALO - 2:HALO - 2 + tb, cs] * cw_ref[0:1, cs]
             + a_ext[HALO - 1:HALO - 1 + tb, cs] * cw_ref[1:2, cs]
             + a_ext[HALO:HALO + tb, cs] * cw_ref[2:3, cs])
        u_scr[:, cs] = (u * _sigmoid(u) * gate).astype(_BF16)
    y = h + _dot(u_scr[...], wdn_ref[...])
    o_ref[0] = y * lax.rsqrt(jnp.mean(y * y, axis=-1, keepdims=True) + EPS) * fw_ref[...]


def _const_spec(shape):
    zeros = (0,) * len(shape)
    return pl.BlockSpec(shape, lambda b, t: zeros, pipeline_mode=pl.Buffered(1))


def _pad_heads(w, n_heads, dk):
    lead = w.shape[:-1]
    w = w.reshape(lead + (n_heads, dk))
    w = jnp.pad(w, [(0, 0)] * len(lead) + [(0, 0), (0, HEAD - dk)])
    return w.reshape(lead + (n_heads * HEAD,))


def kernel(x, attn_norm_w, w_in, hgrn_lower_bounds, w_gla_gate_up, b_gla_gate, hg_norm_w,
           gla_norm_w, w_out, ffn_norm_w, w_up, conv_w, conv_b, w_down, final_norm_w):
    batch, seq, d = x.shape
    assert d == D_MODEL and seq % TIME_BLOCK == 0
    assert w_in.shape[0] == 1, "single-layer problem"

    wi = w_in[0]
    o = 4 * HG_WIDTH
    w_gq = _pad_heads(wi[:, o:o + GLA_KEY_WIDTH] * (GLA_DK ** -0.5), 4, GLA_DK)
    w_gk = _pad_heads(wi[:, o + GLA_KEY_WIDTH:o + 2 * GLA_KEY_WIDTH], 4, GLA_DK)
    o2 = o + 2 * GLA_KEY_WIDTH
    w_gd = jnp.pad(wi[:, o2 + 2 * GLA_WIDTH:], ((0, 0), (0, LANE - GLA_GATE_RANK)))
    win_p = jnp.concatenate([wi[:, :o], w_gq, w_gk, wi[:, o2:o2 + 2 * GLA_WIDTH], w_gd],
                            axis=1).astype(_BF16)
    wgu = jnp.pad(_pad_heads(w_gla_gate_up[0], 4, GLA_DK), ((0, LANE - GLA_GATE_RANK), (0, 0)))
    wgu_hi = wgu.astype(_BF16)
    wgu_lo = (wgu - wgu_hi.astype(_F32)).astype(_BF16)
    bg = _pad_heads(b_gla_gate[0], 4, GLA_DK)[None, :]
    head_nw = jnp.concatenate([hg_norm_w[0], gla_norm_w[0]])[None, :]
    n2_np, mask_np = _level_matrices(CHUNK)
    n2 = jnp.asarray(n2_np, dtype=_BF16)
    masks = jnp.asarray(mask_np)

    params = pltpu.CompilerParams(dimension_semantics=("arbitrary", "arbitrary"),
                                  vmem_limit_bytes=VMEM_LIMIT)
    grid = (batch, seq // TIME_BLOCK)
    act_spec = pl.BlockSpec((1, TIME_BLOCK, D_MODEL), lambda b, t: (b, t, 0))

    h = pl.pallas_call(
        _mixer_kernel,
        grid=grid,
        in_specs=[
            act_spec,
            _const_spec((1, D_MODEL)),
            _const_spec((D_MODEL, PROJ_W)),
            _const_spec(hgrn_lower_bounds.shape),
            _const_spec((LANE, 512)),
            _const_spec((LANE, 512)),
            _const_spec((1, 512)),
            _const_spec((1, MIX)),
            _const_spec((MIX, D_MODEL)),
            _const_spec(n2.shape),
            _const_spec(masks.shape),
        ],
        out_specs=act_spec,
        out_shape=jax.ShapeDtypeStruct(x.shape, _F32),
        scratch_shapes=[
            pltpu.VMEM((TIME_BLOCK, PROJ_W), _F32),
            *[pltpu.VMEM((n2.shape[0], SLAB), _F32)
              for _ in range(MIX // SLAB)],
            pltpu.VMEM((CHUNK, HG_WIDTH), _F32),
            pltpu.VMEM((N_HEADS, CHUNK, CHUNK), _BF16),
            pltpu.VMEM((N_HEADS, HEAD, HEAD), _F32),
            pltpu.VMEM((TIME_BLOCK, MIX), _BF16),
        ],
        compiler_params=params,
        name="mixer",
    )(x, attn_norm_w, win_p, hgrn_lower_bounds, wgu_hi, wgu_lo, bg, head_nw,
      w_out[0].astype(_BF16), n2, masks)

    out = pl.pallas_call(
        _ffn_kernel,
        grid=grid,
        in_specs=[
            act_spec,
            _const_spec((1, D_MODEL)),
            _const_spec((D_MODEL, 2 * D_FF)),
            _const_spec((3, D_FF)),
            _const_spec((1, D_FF)),
            _const_spec((D_FF, D_MODEL)),
            _const_spec((1, D_MODEL)),
        ],
        out_specs=act_spec,
        out_shape=jax.ShapeDtypeStruct(x.shape, _F32),
        scratch_shapes=[
            pltpu.VMEM((HALO + TIME_BLOCK, D_FF), _F32),
            pltpu.VMEM((TIME_BLOCK, D_FF), _BF16),
        ],
        compiler_params=params,
        name="ffn",
    )(h, ffn_norm_w, w_up[0].astype(_BF16), conv_w[0], conv_b, w_down[0].astype(_BF16),
      final_norm_w[None, :])
    return out
```

```python
import numpy as np
import jax
import jax.numpy as jnp
from jax import lax
from jax.experimental import pallas as pl
from jax.experimental.pallas import tpu as pltpu

D_MODEL = 1024
HG_WIDTH = 512
GLA_WIDTH = 512
GLA_DK = 64
GLA_KEY_WIDTH = 256
GLA_GATE_RANK = 16
GLA_GATE_TAU = 16.0
D_FF = 2816
EPS = 1e-6
LOG2_E = 1.4426950408889634

LANE = 128
HEAD = 128
N_HEADS = 8
MIX = N_HEADS * HEAD
SLAB = 2 * LANE
N_SLABS = 3
CHUNK = 128
TIME_BLOCK = 512
FF_TILE = 256
HALO = 8
VMEM_LIMIT = 58 * 1024 * 1024

_HQ, _HF, _HI, _HGT = 0, 512, 1024, 1536
_GQ, _GK, _GV, _GG, _GD = 2048, 2304, 2560, 3072, 3584
PROJ_W = _GD + LANE

_F32 = jnp.float32
_BF16 = jnp.bfloat16


def _level_matrices(c):
    r = np.arange(c)[:, None]
    s = np.arange(c)[None, :]
    blocks = [s <= r, s > r]
    masks = [np.eye(c, dtype=bool)]
    m = 1
    while m < c:
        ref = (r // (2 * m)) * (2 * m) + m - 1
        hi = (r // m) % 2 == 1
        blocks.append(np.where(hi, (s > ref) & (s <= r), (s > r) & (s <= ref)))
        masks.append(((r // (2 * m)) == (s // (2 * m))) & hi & ((s // m) % 2 == 0))
        m *= 2
    n = np.concatenate(blocks, axis=0).astype(np.float32)
    n2 = np.concatenate([n, n], axis=-1)
    return n2, np.stack(masks).astype(np.float32)


def _sigmoid(x):
    return 1.0 / (1.0 + jnp.exp(-x))


def _split_bf16(x):
    hi = x.astype(_BF16)
    lo = (x - hi.astype(_F32)).astype(_BF16)
    return hi, lo


def _dot(a, b):
    return jnp.dot(a, b, preferred_element_type=_F32)


def _dot_nt(a, b):
    return lax.dot_general(a, b, (((1,), (1,)), ((), ())), preferred_element_type=_F32)


def _mixer_kernel(x_ref, nw_ref, win_ref, lb_ref, wgh_ref, wgl_ref, bg_ref, hnw_ref,
                  wout_ref, n2_ref, mask_ref, o_ref,
                  proj_scr, xs0, xs1, xs2, k_scr, a_scr, st_scr, o_scr):
    n_levels = mask_ref.shape[0]
    x_scrs = (xs0, xs1, xs2)

    @pl.when(pl.program_id(1) == 0)
    def _():
        st_scr[...] = jnp.zeros_like(st_scr)

    x = x_ref[0]
    xn = x * lax.rsqrt(jnp.mean(x * x, axis=-1, keepdims=True) + EPS) * nw_ref[...]
    xn = xn.astype(_BF16)
    for c0 in range(0, PROJ_W, 512):
        w = min(512, PROJ_W - c0)
        proj_scr[:, c0:c0 + w] = _dot(xn, win_ref[:, c0:c0 + w])

    p = lb_ref[...]
    pe = jnp.exp(p - jnp.max(p, axis=0, keepdims=True))
    lb = pe[0:1, :] / jnp.sum(pe, axis=0, keepdims=True)

    lane = lax.broadcasted_iota(jnp.int32, (1, LANE), 1)
    half = [(lane < GLA_DK).astype(_BF16), (lane >= GLA_DK).astype(_BF16)]

    def chunk_body(ci, carry):
        rows = pl.ds(pl.multiple_of(ci * CHUNK, CHUNK), CHUNK)

        def load_qk(s):
            if s < 2:
                c0 = s * SLAB
                return proj_scr[rows, _HQ + c0:_HQ + c0 + SLAB], k_scr[:, c0:c0 + SLAB]
            return proj_scr[rows, _GQ:_GQ + SLAB], proj_scr[rows, _GK:_GK + SLAB]

        for s in range(N_SLABS):
            if s < 2:
                c0 = s * SLAB
                lb_s = lb[:, c0:c0 + SLAB]
                f = lb_s + (1.0 - lb_s) * _sigmoid(proj_scr[rows, _HF + c0:_HF + c0 + SLAB])
                g = jnp.log2(f)
                k_scr[:, c0:c0 + SLAB] = 1.0 - f
            else:
                gd_hi, gd_lo = _split_bf16(proj_scr[rows, _GD:_GD + LANE])
                logit = (_dot(gd_hi, wgh_ref[...]) + _dot(gd_lo, wgh_ref[...])
                         + _dot(gd_hi, wgl_ref[...]) + bg_ref[...])
                g = (jnp.minimum(logit, 0.0) * LOG2_E
                     - jnp.log2(1.0 + jnp.exp(-jnp.abs(logit)))) * (1.0 / GLA_GATE_TAU)
            g_hi, g_lo = _split_bf16(g)
            x_scrs[s][...] = _dot(n2_ref[...], jnp.concatenate([g_hi, g_lo], axis=0))

        for s in range(N_SLABS):
            q, k = load_qk(s)
            acc = [None] * 4
            for lv in range(n_levels):
                if lv == 0:
                    qp, kp = q.astype(_BF16), k.astype(_BF16)
                else:
                    e = jnp.exp2(x_scrs[s][(lv + 1) * CHUNK:(lv + 2) * CHUNK, :])
                    qp, kp = (q * e).astype(_BF16), (k * e).astype(_BF16)
                m = mask_ref[lv]
                for j in range(2):
                    js = slice(j * LANE, (j + 1) * LANE)
                    if s < 2:
                        parts = [_dot_nt(qp[:, js], kp[:, js])]
                    else:
                        both = jnp.concatenate([qp[:, js] * half[0], qp[:, js] * half[1]], axis=0)
                        r = _dot_nt(both, kp[:, js])
                        parts = [r[0:CHUNK], r[CHUNK:2 * CHUNK]]
                    for i, part in enumerate(parts):
                        idx = j * len(parts) + i
                        a = part * m
                        acc[idx] = a if acc[idx] is None else acc[idx] + a
            for idx in range(2 if s < 2 else 4):
                a_scr[(2 * s if s < 2 else 4) + idx] = acc[idx].astype(_BF16)

        for s in range(N_SLABS):
            q, k = load_qk(s)
            e_q = jnp.exp2(x_scrs[s][0:CHUNK, :])
            e_k = jnp.exp2(x_scrs[s][CHUNK:2 * CHUNK, :])
            qd = (q * e_q).astype(_BF16)
            kd = (k * e_k).astype(_BF16)
            chunk_decay = e_q[CHUNK - 1:CHUNK, :]
            for idx in range(2 if s < 2 else 4):
                if s < 2:
                    h = 2 * s + idx
                    js = slice(idx * LANE, (idx + 1) * LANE)
                    qd_h, kd_h = qd[:, js], kd[:, js]
                    v_col, gate_col = _HI + h * HEAD, _HGT + h * HEAD
                else:
                    h = 4 + idx
                    js = slice((idx // 2) * LANE, (idx // 2 + 1) * LANE)
                    qd_h, kd_h = qd[:, js] * half[idx % 2], kd[:, js] * half[idx % 2]
                    v_col, gate_col = _GV + idx * HEAD, _GG + idx * HEAD
                hs = slice(h * HEAD, (h + 1) * HEAD)
                st = st_scr[h]
                v_f = proj_scr[rows, v_col:v_col + HEAD]
                o_h = _dot(a_scr[h], v_f.astype(_BF16)) + _dot_nt(qd_h, st.astype(_BF16))
                st_scr[h] = st * chunk_decay[:, js] + _dot(v_f.T.astype(_BF16), kd_h)
                o_h = o_h * lax.rsqrt(jnp.mean(o_h * o_h, axis=-1, keepdims=True) + EPS)
                gate = proj_scr[rows, gate_col:gate_col + HEAD]
                o_h = o_h * hnw_ref[:, hs] * (gate * _sigmoid(gate))
                o_scr[rows, hs] = o_h.astype(_BF16)
        return carry

    lax.fori_loop(0, TIME_BLOCK // CHUNK, chunk_body, 0, unroll=2)

    o_ref[0] = x + _dot(o_scr[...], wout_ref[...])


def _ffn_kernel(h_ref, nw_ref, wup_ref, cw_ref, cb_ref, wdn_ref, fw_ref, o_ref,
                a_ext, u_scr):
    tb = h_ref.shape[1]

    @pl.when(pl.program_id(1) == 0)
    def _():
        a_ext[0:HALO, :] = jnp.zeros((HALO, D_FF), _F32)

    @pl.when(pl.program_id(1) > 0)
    def _():
        a_ext[0:HALO, :] = a_ext[tb:tb + HALO, :]

    h = h_ref[0]
    hn = h * lax.rsqrt(jnp.mean(h * h, axis=-1, keepdims=True) + EPS) * nw_ref[...]
    hn = hn.astype(_BF16)
    for c0 in range(0, D_FF, FF_TILE):
        cs = slice(c0, c0 + FF_TILE)
        a_ext[HALO:HALO + tb, cs] = _dot(hn, wup_ref[:, cs])
        gate = _dot(hn, wup_ref[:, D_FF + c0:D_FF + c0 + FF_TILE])
        u = (cb_ref[:, cs]
             + a_ext[HALO - 2:HALO - 2 + tb, cs] * cw_ref[0:1, cs]
             + a_ext[HALO - 1:HALO - 1 + tb, cs] * cw_ref[1:2, cs]
             + a_ext[HALO:HALO + tb, cs] * cw_ref[2:3, cs])
        u_scr[:, cs] = (u * _sigmoid(u) * gate).astype(_BF16)
    y = h + _dot(u_scr[...], wdn_ref[...])
    o_ref[0] = y * lax.rsqrt(jnp.mean(y * y, axis=-1, keepdims=True) + EPS) * fw_ref[...]


def _const_spec(shape):
    zeros = (0,) * len(shape)
    return pl.BlockSpec(shape, lambda b, t: zeros, pipeline_mode=pl.Buffered(1))


def kernel(x, attn_norm_w, w_in, hgrn_lower_bounds, w_gla_gate_up, b_gla_gate, hg_norm_w,
           gla_norm_w, w_out, ffn_norm_w, w_up, conv_w, conv_b, w_down, final_norm_w):
    batch, seq, d = x.shape
    assert d == D_MODEL and seq % TIME_BLOCK == 0
    assert w_in.shape[0] == 1, "single-layer problem"

    wi = w_in[0]
    o = 4 * HG_WIDTH
    w_gq = wi[:, o:o + GLA_KEY_WIDTH] * (GLA_DK ** -0.5)
    o2 = o + GLA_KEY_WIDTH
    o3 = o2 + GLA_KEY_WIDTH + 2 * GLA_WIDTH
    w_gd = jnp.pad(wi[:, o3:], ((0, 0), (0, LANE - GLA_GATE_RANK)))
    win_p = jnp.concatenate([wi[:, :o], w_gq, wi[:, o2:o3], w_gd], axis=1).astype(_BF16)
    wgu = jnp.pad(w_gla_gate_up[0], ((0, LANE - GLA_GATE_RANK), (0, 0)))
    wgu_hi = wgu.astype(_BF16)
    wgu_lo = (wgu - wgu_hi.astype(_F32)).astype(_BF16)
    head_nw = jnp.concatenate([hg_norm_w[0], gla_norm_w[0]])[None, :]
    n2_np, mask_np = _level_matrices(CHUNK)
    n2 = jnp.asarray(n2_np, dtype=_BF16)
    masks = jnp.asarray(mask_np)

    params = pltpu.CompilerParams(dimension_semantics=("arbitrary", "arbitrary"),
                                  vmem_limit_bytes=VMEM_LIMIT)
    grid = (batch, seq // TIME_BLOCK)
    act_spec = pl.BlockSpec((1, TIME_BLOCK, D_MODEL), lambda b, t: (b, t, 0))

    h = pl.pallas_call(
        _mixer_kernel,
        grid=grid,
        in_specs=[
            act_spec,
            _const_spec((1, D_MODEL)),
            _const_spec((D_MODEL, PROJ_W)),
            _const_spec(hgrn_lower_bounds.shape),
            _const_spec((LANE, GLA_KEY_WIDTH)),
            _const_spec((LANE, GLA_KEY_WIDTH)),
            _const_spec((1, GLA_KEY_WIDTH)),
            _const_spec((1, MIX)),
            _const_spec((MIX, D_MODEL)),
            _const_spec(n2.shape),
            _const_spec(masks.shape),
        ],
        out_specs=act_spec,
        out_shape=jax.ShapeDtypeStruct(x.shape, _F32),
        scratch_shapes=[
            pltpu.VMEM((TIME_BLOCK, PROJ_W), _F32),
            *[pltpu.VMEM((n2.shape[0], SLAB), _F32)
              for _ in range(N_SLABS)],
            pltpu.VMEM((CHUNK, HG_WIDTH), _F32),
            pltpu.VMEM((N_HEADS, CHUNK, CHUNK), _BF16),
            pltpu.VMEM((N_HEADS, HEAD, LANE), _F32),
            pltpu.VMEM((TIME_BLOCK, MIX), _BF16),
        ],
        compiler_params=params,
        name="mixer",
    )(x, attn_norm_w, win_p, hgrn_lower_bounds, wgu_hi, wgu_lo, b_gla_gate, head_nw,
      w_out[0].astype(_BF16), n2, masks)

    out = pl.pallas_call(
        _ffn_kernel,
        grid=grid,
        in_specs=[
            act_spec,
            _const_spec((1, D_MODEL)),
            _const_spec((D_MODEL, 2 * D_FF)),
            _const_spec((3, D_FF)),
            _const_spec((1, D_FF)),
            _const_spec((D_FF, D_MODEL)),
            _const_spec((1, D_MODEL)),
        ],
        out_specs=act_spec,
        out_shape=jax.ShapeDtypeStruct(x.shape, _F32),
        scratch_shapes=[
            pltpu.VMEM((HALO + TIME_BLOCK, D_FF), _F32),
            pltpu.VMEM((TIME_BLOCK, D_FF), _BF16),
        ],
        compiler_params=params,
        name="ffn",
    )(h, ffn_norm_w, w_up[0].astype(_BF16), conv_w[0], conv_b, w_down[0].astype(_BF16),
      final_norm_w[None, :])
    return out
```

```python
import numpy as np
import jax
import jax.numpy as jnp
from jax import lax
from jax.experimental import pallas as pl
from jax.experimental.pallas import tpu as pltpu

D_MODEL = 1024
HG_WIDTH = 512
GLA_WIDTH = 512
GLA_DK = 64
GLA_KEY_WIDTH = 256
GLA_GATE_RANK = 16
GLA_GATE_TAU = 16.0
D_FF = 2816
EPS = 1e-6
LOG2_E = 1.4426950408889634

LANE = 128
HEAD = 128
N_HEADS = 8
MIX = N_HEADS * HEAD
SLAB = 2 * LANE
N_SLABS = 3
CHUNK = 128
TIME_BLOCK = 512
FF_TILE = 256
HALO = 8
VMEM_LIMIT = 58 * 1024 * 1024

_HQ, _HF, _HI, _HGT = 0, 512, 1024, 1536
_GQ, _GK, _GV, _GG, _GD = 2048, 2304, 2560, 3072, 3584
PROJ_W = _GD + LANE

_F32 = jnp.float32
_BF16 = jnp.bfloat16


def _level_matrices(c):
    r = np.arange(c)[:, None]
    s = np.arange(c)[None, :]
    blocks = [s <= r, s > r]
    masks = [np.eye(c, dtype=bool)]
    m = 1
    while m < c:
        ref = (r // (2 * m)) * (2 * m) + m - 1
        hi = (r // m) % 2 == 1
        blocks.append(np.where(hi, (s > ref) & (s <= r), (s > r) & (s <= ref)))
        masks.append(((r // (2 * m)) == (s // (2 * m))) & hi & ((s // m) % 2 == 0))
        m *= 2
    n = np.concatenate(blocks, axis=0).astype(np.float32)
    n2 = np.concatenate([n, n], axis=-1)
    return n2, np.stack(masks).astype(np.float32)


def _sigmoid(x):
    return 1.0 / (1.0 + jnp.exp(-x))


def _split_bf16(x):
    hi = x.astype(_BF16)
    lo = (x - hi.astype(_F32)).astype(_BF16)
    return hi, lo


def _dot(a, b):
    return jnp.dot(a, b, preferred_element_type=_F32)


def _dot_nt(a, b):
    return lax.dot_general(a, b, (((1,), (1,)), ((), ())), preferred_element_type=_F32)


def _mixer_kernel(x_ref, nw_ref, win_ref, lb_ref, wgh_ref, wgl_ref, bg_ref, hnw_ref,
                  wout_ref, n2_ref, mask_ref, o_ref,
                  proj_scr, xs0, xs1, xs2, k_scr, a_scr, st_scr, o_scr):
    n_levels = mask_ref.shape[0]
    x_scrs = (xs0, xs1, xs2)

    @pl.when(pl.program_id(1) == 0)
    def _():
        st_scr[...] = jnp.zeros_like(st_scr)

    x = x_ref[0]
    xn = x * lax.rsqrt(jnp.mean(x * x, axis=-1, keepdims=True) + EPS) * nw_ref[...]
    xn = xn.astype(_BF16)
    for c0 in range(0, _GD, 512):
        proj_scr[:, c0:c0 + 512] = _dot_nt(xn, win_ref[c0:c0 + 512, :].astype(_BF16))
    proj_scr[:, _GD:_GD + GLA_GATE_RANK] = _dot_nt(
        xn, win_ref[_GD:_GD + GLA_GATE_RANK, :].astype(_BF16))

    p = lb_ref[...]
    pe = jnp.exp(p - jnp.max(p, axis=0, keepdims=True))
    lb = pe[0:1, :] / jnp.sum(pe, axis=0, keepdims=True)

    lane = lax.broadcasted_iota(jnp.int32, (1, LANE), 1)
    half = [(lane < GLA_DK).astype(_BF16), (lane >= GLA_DK).astype(_BF16)]

    def chunk_body(ci, carry):
        rows = pl.ds(pl.multiple_of(ci * CHUNK, CHUNK), CHUNK)

        def load_qk(s):
            if s < 2:
                c0 = s * SLAB
                return proj_scr[rows, _HQ + c0:_HQ + c0 + SLAB], k_scr[:, c0:c0 + SLAB]
            q = proj_scr[rows, _GQ:_GQ + SLAB] * (GLA_DK ** -0.5)
            return q, proj_scr[rows, _GK:_GK + SLAB]

        for s in range(N_SLABS):
            if s < 2:
                c0 = s * SLAB
                lb_s = lb[:, c0:c0 + SLAB]
                f = lb_s + (1.0 - lb_s) * _sigmoid(proj_scr[rows, _HF + c0:_HF + c0 + SLAB])
                g = jnp.log2(f)
                k_scr[:, c0:c0 + SLAB] = 1.0 - f
            else:
                gd_hi, gd_lo = _split_bf16(proj_scr[rows, _GD:_GD + GLA_GATE_RANK])
                logit = (_dot(gd_hi, wgh_ref[...]) + _dot(gd_lo, wgh_ref[...])
                         + _dot(gd_hi, wgl_ref[...]) + bg_ref[...])
                g = (jnp.minimum(logit, 0.0) * LOG2_E
                     - jnp.log2(1.0 + jnp.exp(-jnp.abs(logit)))) * (1.0 / GLA_GATE_TAU)
            g_hi, g_lo = _split_bf16(g)
            x_scrs[s][...] = _dot(n2_ref[...], jnp.concatenate([g_hi, g_lo], axis=0))

        for s in range(N_SLABS):
            q, k = load_qk(s)
            acc = [None] * 4
            for lv in range(n_levels):
                if lv == 0:
                    qp, kp = q.astype(_BF16), k.astype(_BF16)
                else:
                    e = jnp.exp2(x_scrs[s][(lv + 1) * CHUNK:(lv + 2) * CHUNK, :])
                    qp, kp = (q * e).astype(_BF16), (k * e).astype(_BF16)
                m = mask_ref[lv]
                for j in range(2):
                    js = slice(j * LANE, (j + 1) * LANE)
                    if s < 2:
                        parts = [_dot_nt(qp[:, js], kp[:, js])]
                    else:
                        both = jnp.concatenate([qp[:, js] * half[0], qp[:, js] * half[1]], axis=0)
                        r = _dot_nt(both, kp[:, js])
                        parts = [r[0:CHUNK], r[CHUNK:2 * CHUNK]]
                    for i, part in enumerate(parts):
                        idx = j * len(parts) + i
                        a = part * m
                        acc[idx] = a if acc[idx] is None else acc[idx] + a
            for idx in range(2 if s < 2 else 4):
                a_scr[(2 * s if s < 2 else 4) + idx] = acc[idx].astype(_BF16)

        for s in range(N_SLABS):
            q, k = load_qk(s)
            e_q = jnp.exp2(x_scrs[s][0:CHUNK, :])
            e_k = jnp.exp2(x_scrs[s][CHUNK:2 * CHUNK, :])
            qd = (q * e_q).astype(_BF16)
            kd = (k * e_k).astype(_BF16)
            chunk_decay = e_q[CHUNK - 1:CHUNK, :]
            for idx in range(2 if s < 2 else 4):
                if s < 2:
                    h = 2 * s + idx
                    js = slice(idx * LANE, (idx + 1) * LANE)
                    qd_h, kd_h = qd[:, js], kd[:, js]
                    v_col, gate_col = _HI + h * HEAD, _HGT + h * HEAD
                else:
                    h = 4 + idx
                    js = slice((idx // 2) * LANE, (idx // 2 + 1) * LANE)
                    qd_h, kd_h = qd[:, js] * half[idx % 2], kd[:, js] * half[idx % 2]
                    v_col, gate_col = _GV + idx * HEAD, _GG + idx * HEAD
                hs = slice(h * HEAD, (h + 1) * HEAD)
                st = st_scr[h]
                v_f = proj_scr[rows, v_col:v_col + HEAD]
                o_h = _dot(a_scr[h], v_f.astype(_BF16)) + _dot_nt(qd_h, st.astype(_BF16))
                st_scr[h] = st * chunk_decay[:, js] + _dot(v_f.T.astype(_BF16), kd_h)
                o_h = o_h * lax.rsqrt(jnp.mean(o_h * o_h, axis=-1, keepdims=True) + EPS)
                gate = proj_scr[rows, gate_col:gate_col + HEAD]
                o_h = o_h * hnw_ref[:, hs] * (gate * _sigmoid(gate))
                o_scr[rows, hs] = o_h.astype(_BF16)
        return carry

    lax.fori_loop(0, TIME_BLOCK // CHUNK, chunk_body, 0, unroll=2)

    o_ref[0] = x + _dot(o_scr[...], wout_ref[...])


def _ffn_kernel(h_ref, nw_ref, wup_ref, cw_ref, cb_ref, wdn_ref, fw_ref, o_ref,
                a_ext, u_scr):
    tb = h_ref.shape[1]

    @pl.when(pl.program_id(1) == 0)
    def _():
        a_ext[0:HALO, :] = jnp.zeros((HALO, D_FF), _F32)

    @pl.when(pl.program_id(1) > 0)
    def _():
        a_ext[0:HALO, :] = a_ext[tb:tb + HALO, :]

    h = h_ref[0]
    hn = h * lax.rsqrt(jnp.mean(h * h, axis=-1, keepdims=True) + EPS) * nw_ref[...]
    hn = hn.astype(_BF16)
    for c0 in range(0, D_FF, FF_TILE):
        cs = slice(c0, c0 + FF_TILE)
        a_ext[HALO:HALO + tb, cs] = _dot(hn, wup_ref[:, cs])
        gate = _dot(hn, wup_ref[:, D_FF + c0:D_FF + c0 + FF_TILE])
        u = (cb_ref[:, cs]
             + a_ext[HALO - 2:HALO - 2 + tb, cs] * cw_ref[0:1, cs]
             + a_ext[HALO - 1:HALO - 1 + tb, cs] * cw_ref[1:2, cs]
             + a_ext[HALO:HALO + tb, cs] * cw_ref[2:3, cs])
        u_scr[:, cs] = (u * _sigmoid(u) * gate).astype(_BF16)
    y = h + _dot(u_scr[...], wdn_ref[...])
    o_ref[0] = y * lax.rsqrt(jnp.mean(y * y, axis=-1, keepdims=True) + EPS) * fw_ref[...]


def _const_spec(shape):
    zeros = (0,) * len(shape)
    return pl.BlockSpec(shape, lambda b, t: zeros, pipeline_mode=pl.Buffered(1))


def kernel(x, attn_norm_w, w_in, hgrn_lower_bounds, w_gla_gate_up, b_gla_gate, hg_norm_w,
           gla_norm_w, w_out, ffn_norm_w, w_up, conv_w, conv_b, w_down, final_norm_w):
    batch, seq, d = x.shape
    assert d == D_MODEL and seq % TIME_BLOCK == 0
    assert w_in.shape[0] == 1, "single-layer problem"

    assert w_in.shape[2] == _GD + GLA_GATE_RANK
    win_t = jnp.transpose(w_in[0])
    wgu = w_gla_gate_up[0]
    wgu_hi = wgu.astype(_BF16)
    wgu_lo = (wgu - wgu_hi.astype(_F32)).astype(_BF16)
    head_nw = jnp.concatenate([hg_norm_w[0], gla_norm_w[0]])[None, :]
    n2_np, mask_np = _level_matrices(CHUNK)
    n2 = jnp.asarray(n2_np, dtype=_BF16)
    masks = jnp.asarray(mask_np)

    params = pltpu.CompilerParams(dimension_semantics=("arbitrary", "arbitrary"),
                                  vmem_limit_bytes=VMEM_LIMIT)
    grid = (batch, seq // TIME_BLOCK)
    act_spec = pl.BlockSpec((1, TIME_BLOCK, D_MODEL), lambda b, t: (b, t, 0))

    h = pl.pallas_call(
        _mixer_kernel,
        grid=grid,
        in_specs=[
            act_spec,
            _const_spec((1, D_MODEL)),
            _const_spec(win_t.shape),
            _const_spec(hgrn_lower_bounds.shape),
            _const_spec((GLA_GATE_RANK, GLA_KEY_WIDTH)),
            _const_spec((GLA_GATE_RANK, GLA_KEY_WIDTH)),
            _const_spec((1, GLA_KEY_WIDTH)),
            _const_spec((1, MIX)),
            _const_spec((MIX, D_MODEL)),
            _const_spec(n2.shape),
            _const_spec(masks.shape),
        ],
        out_specs=act_spec,
        out_shape=jax.ShapeDtypeStruct(x.shape, _F32),
        scratch_shapes=[
            pltpu.VMEM((TIME_BLOCK, PROJ_W), _F32),
            *[pltpu.VMEM((n2.shape[0], SLAB), _F32)
              for _ in range(N_SLABS)],
            pltpu.VMEM((CHUNK, HG_WIDTH), _F32),
            pltpu.VMEM((N_HEADS, CHUNK, CHUNK), _BF16),
            pltpu.VMEM((N_HEADS, HEAD, LANE), _F32),
            pltpu.VMEM((TIME_BLOCK, MIX), _BF16),
        ],
        compiler_params=params,
        name="mixer",
    )(x, attn_norm_w, win_t, hgrn_lower_bounds, wgu_hi, wgu_lo, b_gla_gate, head_nw,
      w_out[0].astype(_BF16), n2, masks)

    out = pl.pallas_call(
        _ffn_kernel,
        grid=grid,
        in_specs=[
            act_spec,
            _const_spec((1, D_MODEL)),
            _const_spec((D_MODEL, 2 * D_FF)),
            _const_spec((3, D_FF)),
            _const_spec((1, D_FF)),
            _const_spec((D_FF, D_MODEL)),
            _const_spec((1, D_MODEL)),
        ],
        out_specs=act_spec,
        out_shape=jax.ShapeDtypeStruct(x.shape, _F32),
        scratch_shapes=[
            pltpu.VMEM((HALO + TIME_BLOCK, D_FF), _F32),
            pltpu.VMEM((TIME_BLOCK, D_FF), _BF16),
        ],
        compiler_params=params,
        name="ffn",
    )(h, ffn_norm_w, w_up[0].astype(_BF16), conv_w[0], conv_b, w_down[0].astype(_BF16),
      final_norm_w[None, :])
    return out
```

```python
import numpy as np
import jax
import jax.numpy as jnp
from jax import lax
from jax.experimental import pallas as pl
from jax.experimental.pallas import tpu as pltpu

D_MODEL = 1024
HG_WIDTH = 512
GLA_WIDTH = 512
GLA_DK = 64
GLA_KEY_WIDTH = 256
GLA_GATE_RANK = 16
GLA_GATE_TAU = 16.0
D_FF = 2816
EPS = 1e-6
LOG2_E = 1.4426950408889634

LANE = 128
HEAD = 128
N_HEADS = 8
MIX = N_HEADS * HEAD
SLAB = 2 * LANE
N_SLABS = 3
CHUNK = 128
MIXER_BLOCK = 512
FFN_BLOCK = 512
FF_TILE = 256
SUBLANES = 8
HALO = SUBLANES
VMEM_LIMIT = 58 * 1024 * 1024

_HQ, _HF, _HI, _HGT = 0, 512, 1024, 1536
_GQ, _GK, _GV, _GG, _GD = 2048, 2304, 2560, 3072, 3584
PROJ_W = _GD + LANE

_F32 = jnp.float32
_BF16 = jnp.bfloat16


def _level_matrices(c):
    r = np.arange(c)[:, None]
    s = np.arange(c)[None, :]
    blocks = [s <= r]
    masks = [np.eye(c, dtype=bool)]
    m = 1
    while m < c:
        ref = (r // (2 * m)) * (2 * m) + m - 1
        hi = (r // m) % 2 == 1
        if m < SUBLANES:
            blocks.append(np.where(hi, (s > ref) & (s <= r), (s > r) & (s <= ref)))
        masks.append(((r // (2 * m)) == (s // (2 * m))) & hi & ((s // m) % 2 == 0))
        m *= 2
    n = np.concatenate(blocks, axis=0).astype(np.float32)
    n2 = np.concatenate([n, n], axis=-1)
    return n2, np.stack(masks).astype(np.float32)


def _sigmoid(x):
    return 1.0 / (1.0 + jnp.exp(-x))


def _split_bf16(x):
    hi = x.astype(_BF16)
    lo = (x - hi.astype(_F32)).astype(_BF16)
    return hi, lo


def _dot(a, b):
    return jnp.dot(a, b, preferred_element_type=_F32)


def _dot_nt(a, b):
    return lax.dot_general(a, b, (((1,), (1,)), ((), ())), preferred_element_type=_F32)


def _mixer_kernel(x_ref, nw_ref, win_ref, lb_ref, wgh_ref, wgl_ref, bg_ref, hnw_ref,
                  wout_ref, n2_ref, mask_ref, o_ref,
                  proj_scr, xs0, xs1, xs2, k_scr, a_scr, st_scr, o_scr):
    n_levels = mask_ref.shape[0]
    x_scrs = (xs0, xs1, xs2)

    @pl.when(pl.program_id(1) == 0)
    def _():
        st_scr[...] = jnp.zeros_like(st_scr)

    x = x_ref[0]
    xn = x * lax.rsqrt(jnp.mean(x * x, axis=-1, keepdims=True) + EPS) * nw_ref[...]
    xn = xn.astype(_BF16)
    for c0 in range(0, _GD, 512):
        proj_scr[:, c0:c0 + 512] = _dot_nt(xn, win_ref[c0:c0 + 512, :].astype(_BF16))
    proj_scr[:, _GD:_GD + GLA_GATE_RANK] = _dot_nt(
        xn, win_ref[_GD:_GD + GLA_GATE_RANK, :].astype(_BF16))

    p = lb_ref[...]
    pe = jnp.exp(p - jnp.max(p, axis=0, keepdims=True))
    lb = pe[0:1, :] / jnp.sum(pe, axis=0, keepdims=True)

    lane = lax.broadcasted_iota(jnp.int32, (1, LANE), 1)
    half = [(lane < GLA_DK).astype(_BF16), (lane >= GLA_DK).astype(_BF16)]

    def chunk_body(ci, carry):
        rows = pl.ds(pl.multiple_of(ci * CHUNK, CHUNK), CHUNK)

        def load_qk(s):
            if s < 2:
                c0 = s * SLAB
                return proj_scr[rows, _HQ + c0:_HQ + c0 + SLAB], k_scr[:, c0:c0 + SLAB]
            q = proj_scr[rows, _GQ:_GQ + SLAB] * (GLA_DK ** -0.5)
            return q, proj_scr[rows, _GK:_GK + SLAB]

        for s in range(N_SLABS):
            if s < 2:
                c0 = s * SLAB
                lb_s = lb[:, c0:c0 + SLAB]
                f = lb_s + (1.0 - lb_s) * _sigmoid(proj_scr[rows, _HF + c0:_HF + c0 + SLAB])
                g = jnp.log2(f)
                k_scr[:, c0:c0 + SLAB] = 1.0 - f
            else:
                gd_hi, gd_lo = _split_bf16(proj_scr[rows, _GD:_GD + GLA_GATE_RANK])
                logit = (_dot(gd_hi, wgh_ref[...]) + _dot(gd_lo, wgh_ref[...])
                         + _dot(gd_hi, wgl_ref[...]) + bg_ref[...])
                g = (jnp.minimum(logit, 0.0) * LOG2_E
                     - jnp.log2(1.0 + jnp.exp(-jnp.abs(logit)))) * (1.0 / GLA_GATE_TAU)
            g_hi, g_lo = _split_bf16(g)
            x_scrs[s][...] = _dot(n2_ref[...], jnp.concatenate([g_hi, g_lo], axis=0))

        def level_exponent(s, lv):
            m = 1 << (lv - 1)
            if m < SUBLANES:
                return x_scrs[s][lv * CHUNK:(lv + 1) * CHUNK, :]
            cum = x_scrs[s][0:CHUNK, :]
            ref = jnp.concatenate(
                [jnp.broadcast_to(cum[r:r + 1, :], (2 * m, SLAB))
                 for r in range(m - 1, CHUNK, 2 * m)], axis=0)
            return -jnp.abs(cum - ref)

        for s in range(N_SLABS):
            q, k = load_qk(s)
            q, k = q.astype(_BF16), k.astype(_BF16)
            acc = [None] * 4
            for lv in range(n_levels):
                if lv == 0:
                    qp, kp = q, k
                else:
                    e = jnp.exp2(level_exponent(s, lv)).astype(_BF16)
                    qp, kp = q * e, k * e
                m = mask_ref[lv]
                for j in range(2):
                    js = slice(j * LANE, (j + 1) * LANE)
                    if s < 2:
                        parts = [_dot_nt(qp[:, js], kp[:, js])]
                    else:
                        both = jnp.concatenate([qp[:, js] * half[0], qp[:, js] * half[1]], axis=0)
                        r = _dot_nt(both, kp[:, js])
                        parts = [r[0:CHUNK], r[CHUNK:2 * CHUNK]]
                    for i, part in enumerate(parts):
                        idx = j * len(parts) + i
                        a = part * m
                        acc[idx] = a if acc[idx] is None else acc[idx] + a
            for idx in range(2 if s < 2 else 4):
                a_scr[(2 * s if s < 2 else 4) + idx] = acc[idx].astype(_BF16)

        for s in range(N_SLABS):
            q, k = load_qk(s)
            cum = x_scrs[s][0:CHUNK, :]
            e_q = jnp.exp2(cum)
            e_k = jnp.exp2(cum[CHUNK - 1:CHUNK, :] - cum)
            qd = (q * e_q).astype(_BF16)
            kd = (k * e_k).astype(_BF16)
            chunk_decay = e_q[CHUNK - 1:CHUNK, :]
            for idx in range(2 if s < 2 else 4):
                if s < 2:
                    h = 2 * s + idx
                    js = slice(idx * LANE, (idx + 1) * LANE)
                    qd_h, kd_h = qd[:, js], kd[:, js]
                    v_col, gate_col = _HI + h * HEAD, _HGT + h * HEAD
                else:
                    h = 4 + idx
                    js = slice((idx // 2) * LANE, (idx // 2 + 1) * LANE)
                    qd_h, kd_h = qd[:, js] * half[idx % 2], kd[:, js] * half[idx % 2]
                    v_col, gate_col = _GV + idx * HEAD, _GG + idx * HEAD
                hs = slice(h * HEAD, (h + 1) * HEAD)
                st = st_scr[h]
                v_f = proj_scr[rows, v_col:v_col + HEAD]
                o_h = _dot(a_scr[h], v_f.astype(_BF16)) + _dot_nt(qd_h, st.astype(_BF16))
                st_scr[h] = st * chunk_decay[:, js] + _dot(v_f.T.astype(_BF16), kd_h)
                o_h = o_h * lax.rsqrt(jnp.mean(o_h * o_h, axis=-1, keepdims=True) + EPS)
                gate = proj_scr[rows, gate_col:gate_col + HEAD]
                o_h = o_h * hnw_ref[:, hs] * (gate * _sigmoid(gate))
                o_scr[rows, hs] = o_h.astype(_BF16)
        return carry

    lax.fori_loop(0, MIXER_BLOCK // CHUNK, chunk_body, 0, unroll=2)

    o_ref[0] = x + _dot(o_scr[...], wout_ref[...])


def _ffn_kernel(h_ref, nw_ref, wup_ref, cw_ref, cb_ref, wdn_ref, fw_ref, o_ref,
                a_ext, u_scr):
    tb = h_ref.shape[1]

    @pl.when(pl.program_id(1) == 0)
    def _():
        a_ext[0:HALO, :] = jnp.zeros((HALO, D_FF), _F32)

    @pl.when(pl.program_id(1) > 0)
    def _():
        a_ext[0:HALO, :] = a_ext[tb:tb + HALO, :]

    h = h_ref[0]
    hn = h * lax.rsqrt(jnp.mean(h * h, axis=-1, keepdims=True) + EPS) * nw_ref[...]
    hn = hn.astype(_BF16)
    for c0 in range(0, D_FF, FF_TILE):
        cs = slice(c0, c0 + FF_TILE)
        a_ext[HALO:HALO + tb, cs] = _dot(hn, wup_ref[:, cs])
        gate = _dot(hn, wup_ref[:, D_FF + c0:D_FF + c0 + FF_TILE])
        u = (cb_ref[:, cs]
             + a_ext[HALO - 2:HALO - 2 + tb, cs] * cw_ref[0:1, cs]
             + a_ext[HALO - 1:HALO - 1 + tb, cs] * cw_ref[1:2, cs]
             + a_ext[HALO:HALO + tb, cs] * cw_ref[2:3, cs])
        u_scr[:, cs] = (u * _sigmoid(u) * gate).astype(_BF16)
    y = h + _dot(u_scr[...], wdn_ref[...])
    o_ref[0] = y * lax.rsqrt(jnp.mean(y * y, axis=-1, keepdims=True) + EPS) * fw_ref[...]


def _const_spec(shape):
    zeros = (0,) * len(shape)
    return pl.BlockSpec(shape, lambda b, t: zeros, pipeline_mode=pl.Buffered(1))


def kernel(x, attn_norm_w, w_in, hgrn_lower_bounds, w_gla_gate_up, b_gla_gate, hg_norm_w,
           gla_norm_w, w_out, ffn_norm_w, w_up, conv_w, conv_b, w_down, final_norm_w):
    batch, seq, d = x.shape
    assert d == D_MODEL and seq % MIXER_BLOCK == 0 and seq % FFN_BLOCK == 0
    assert w_in.shape[0] == 1, "single-layer problem"

    assert w_in.shape[2] == _GD + GLA_GATE_RANK
    win_t = jnp.transpose(w_in[0])
    wgu = w_gla_gate_up[0]
    wgu_hi = wgu.astype(_BF16)
    wgu_lo = (wgu - wgu_hi.astype(_F32)).astype(_BF16)
    head_nw = jnp.concatenate([hg_norm_w[0], gla_norm_w[0]])[None, :]
    n2_np, mask_np = _level_matrices(CHUNK)
    n2 = jnp.asarray(n2_np, dtype=_BF16)
    masks = jnp.asarray(mask_np)

    params = pltpu.CompilerParams(dimension_semantics=("arbitrary", "arbitrary"),
                                  vmem_limit_bytes=VMEM_LIMIT)

    def act_spec(block):
        return pl.BlockSpec((1, block, D_MODEL), lambda b, t: (b, t, 0))

    h = pl.pallas_call(
        _mixer_kernel,
        grid=(batch, seq // MIXER_BLOCK),
        in_specs=[
            act_spec(MIXER_BLOCK),
            _const_spec((1, D_MODEL)),
            _const_spec(win_t.shape),
            _const_spec(hgrn_lower_bounds.shape),
            _const_spec((GLA_GATE_RANK, GLA_KEY_WIDTH)),
            _const_spec((GLA_GATE_RANK, GLA_KEY_WIDTH)),
            _const_spec((1, GLA_KEY_WIDTH)),
            _const_spec((1, MIX)),
            _const_spec((MIX, D_MODEL)),
            _const_spec(n2.shape),
            _const_spec(masks.shape),
        ],
        out_specs=act_spec(MIXER_BLOCK),
        out_shape=jax.ShapeDtypeStruct(x.shape, _F32),
        scratch_shapes=[
            pltpu.VMEM((MIXER_BLOCK, PROJ_W), _F32),
            *[pltpu.VMEM((n2.shape[0], SLAB), _F32)
              for _ in range(N_SLABS)],
            pltpu.VMEM((CHUNK, HG_WIDTH), _F32),
            pltpu.VMEM((N_HEADS, CHUNK, CHUNK), _BF16),
            pltpu.VMEM((N_HEADS, HEAD, LANE), _F32),
            pltpu.VMEM((MIXER_BLOCK, MIX), _BF16),
        ],
        compiler_params=params,
        name="mixer",
    )(x, attn_norm_w, win_t, hgrn_lower_bounds, wgu_hi, wgu_lo, b_gla_gate, head_nw,
      w_out[0].astype(_BF16), n2, masks)

    out = pl.pallas_call(
        _ffn_kernel,
        grid=(batch, seq // FFN_BLOCK),
        in_specs=[
            act_spec(FFN_BLOCK),
            _const_spec((1, D_MODEL)),
            _const_spec((D_MODEL, 2 * D_FF)),
            _const_spec((3, D_FF)),
            _const_spec((1, D_FF)),
            _const_spec((D_FF, D_MODEL)),
            _const_spec((1, D_MODEL)),
        ],
        out_specs=act_spec(FFN_BLOCK),
        out_shape=jax.ShapeDtypeStruct(x.shape, _F32),
        scratch_shapes=[
            pltpu.VMEM((HALO + FFN_BLOCK, D_FF), _F32),
            pltpu.VMEM((FFN_BLOCK, D_FF), _BF16),
        ],
        compiler_params=params,
        name="ffn",
    )(h, ffn_norm_w, w_up[0].astype(_BF16), conv_w[0], conv_b, w_down[0].astype(_BF16),
      final_norm_w[None, :])
    return out
```

```python
import functools

import numpy as np
import jax
import jax.numpy as jnp
from jax import lax
from jax.experimental import pallas as pl
from jax.experimental.pallas import tpu as pltpu

D_MODEL = 1024
HG_WIDTH = 512
GLA_WIDTH = 512
GLA_DK = 64
GLA_KEY_WIDTH = 256
GLA_GATE_RANK = 16
GLA_GATE_TAU = 16.0
D_FF = 2816
EPS = 1e-6
LOG2_E = 1.4426950408889634

LANE = 128
HEAD = 128
N_HEADS = 8
MIX = N_HEADS * HEAD
SLAB = 2 * LANE
N_SLABS = 3
CHUNK = 128
MIXER_BLOCK = 512
FFN_BLOCK = 512
PROJ_TILE = 512
FF_TILE = 256
SUBLANES = 8
HALO = SUBLANES
VMEM_LIMIT = 58 * 1024 * 1024

_HQ, _HF, _HI, _HGT = 0, 512, 1024, 1536
_GQ, _GK, _GV, _GG, _GD = 2048, 2304, 2560, 3072, 3584
_PF_W = HG_WIDTH + LANE

_F32 = jnp.float32
_BF16 = jnp.bfloat16


def _level_matrices(c):
    r = np.arange(c)[:, None]
    s = np.arange(c)[None, :]
    blocks = [s <= r]
    masks = [np.eye(c, dtype=bool)]
    m = 1
    while m < c:
        ref = (r // (2 * m)) * (2 * m) + m - 1
        hi = (r // m) % 2 == 1
        if m < SUBLANES:
            blocks.append(np.where(hi, (s > ref) & (s <= r), (s > r) & (s <= ref)))
        masks.append(((r // (2 * m)) == (s // (2 * m))) & hi & ((s // m) % 2 == 0))
        m *= 2
    n = np.concatenate(blocks, axis=0).astype(np.float32)
    n2 = np.concatenate([n, n], axis=-1)
    return n2, np.stack(masks).astype(np.float32)


def _sigmoid(x):
    return 1.0 / (1.0 + jnp.exp(-x))


def _split_bf16(x):
    hi = x.astype(_BF16)
    lo = (x - hi.astype(_F32)).astype(_BF16)
    return hi, lo


def _dot(a, b):
    return jnp.dot(a, b, preferred_element_type=_F32)


def _dot_nt(a, b):
    return lax.dot_general(a, b, (((1,), (1,)), ((), ())), preferred_element_type=_F32)


def _mixer_kernel(steps_per_row, xc_ref, xn_ref, nw_ref, win_ref, lb_ref, wgh_ref, wgl_ref,
                  bg_ref, hnw_ref, wout_ref, n2_ref, mask_ref, o_ref,
                  pf_scr, pb_scr, xs0, xs1, xs2, k_scr, a_scr, st_scr, o_scr):
    n_levels = mask_ref.shape[0]
    x_scrs = (xs0, xs1, xs2)
    step = pl.program_id(0)
    cur = step % 2
    nxt = 1 - cur

    def normed(x):
        xn = x * lax.rsqrt(jnp.mean(x * x, axis=-1, keepdims=True) + EPS) * nw_ref[...]
        return xn.astype(_BF16)

    n_proj_tiles = _GD // PROJ_TILE + 1

    def project_tile(xn, t, slot):
        c0 = t * PROJ_TILE
        if c0 == _GD:
            pf_scr[slot, :, HG_WIDTH:HG_WIDTH + GLA_GATE_RANK] = _dot_nt(
                xn, win_ref[_GD:_GD + GLA_GATE_RANK, :].astype(_BF16))
            return
        res = _dot_nt(xn, win_ref[c0:c0 + PROJ_TILE, :].astype(_BF16))
        if c0 == _HF:
            pf_scr[slot, :, 0:HG_WIDTH] = res
        else:
            pb_scr[slot, :, c0:c0 + PROJ_TILE] = res.astype(_BF16)

    @pl.when(step == 0)
    def _():
        xn0 = normed(xc_ref[0])
        for t in range(n_proj_tiles):
            project_tile(xn0, t, 0)

    @pl.when(step % steps_per_row == 0)
    def _():
        st_scr[...] = jnp.zeros_like(st_scr)

    xn_next = normed(xn_ref[0])

    p = lb_ref[...]
    pe = jnp.exp(p - jnp.max(p, axis=0, keepdims=True))
    lb = pe[0:1, :] / jnp.sum(pe, axis=0, keepdims=True)

    lane = lax.broadcasted_iota(jnp.int32, (1, LANE), 1)
    half = [(lane < GLA_DK).astype(_BF16), (lane >= GLA_DK).astype(_BF16)]

    def pb(r0, col, width):
        return pb_scr[cur, r0:r0 + CHUNK, col:col + width]

    def load_qk(r0, s):
        if s < 2:
            c0 = s * SLAB
            return pb(r0, _HQ + c0, SLAB), k_scr[:, c0:c0 + SLAB]
        return pb(r0, _GQ, SLAB) * (GLA_DK ** -0.5), pb(r0, _GK, SLAB)

    def decay_sums(r0):
        for s in range(N_SLABS):
            if s < 2:
                c0 = s * SLAB
                lb_s = lb[:, c0:c0 + SLAB]
                f = lb_s + (1.0 - lb_s) * _sigmoid(pf_scr[cur, r0:r0 + CHUNK, c0:c0 + SLAB])
                g = jnp.log2(f)
                k_scr[:, c0:c0 + SLAB] = (1.0 - f).astype(_BF16)
            else:
                gd_hi, gd_lo = _split_bf16(
                    pf_scr[cur, r0:r0 + CHUNK, HG_WIDTH:HG_WIDTH + GLA_GATE_RANK])
                logit = (_dot(gd_hi, wgh_ref[...]) + _dot(gd_lo, wgh_ref[...])
                         + _dot(gd_hi, wgl_ref[...]) + bg_ref[...])
                g = (jnp.minimum(logit, 0.0) * LOG2_E
                     - jnp.log2(1.0 + jnp.exp(-jnp.abs(logit)))) * (1.0 / GLA_GATE_TAU)
            g_hi, g_lo = _split_bf16(g)
            x_scrs[s][...] = _dot(n2_ref[...], jnp.concatenate([g_hi, g_lo], axis=0))

    def level_exponent(s, lv):
        m = 1 << (lv - 1)
        if m < SUBLANES:
            return x_scrs[s][lv * CHUNK:(lv + 1) * CHUNK, :]
        cum = x_scrs[s][0:CHUNK, :]
        ref = jnp.concatenate(
            [jnp.broadcast_to(cum[r:r + 1, :], (2 * m, SLAB))
             for r in range(m - 1, CHUNK, 2 * m)], axis=0)
        return -jnp.abs(cum - ref)

    def intra_scores(r0):
        for s in range(N_SLABS):
            q, k = load_qk(r0, s)
            acc = [None] * 4
            for lv in range(n_levels):
                if lv == 0:
                    qp, kp = q, k
                else:
                    e = jnp.exp2(level_exponent(s, lv)).astype(_BF16)
                    qp, kp = q * e, k * e
                m = mask_ref[lv]
                for j in range(2):
                    js = slice(j * LANE, (j + 1) * LANE)
                    if s < 2:
                        parts = [_dot_nt(qp[:, js], kp[:, js])]
                    else:
                        both = jnp.concatenate([qp[:, js] * half[0], qp[:, js] * half[1]], axis=0)
                        r = _dot_nt(both, kp[:, js])
                        parts = [r[0:CHUNK], r[CHUNK:2 * CHUNK]]
                    for i, part in enumerate(parts):
                        idx = j * len(parts) + i
                        a = part * m
                        acc[idx] = a if acc[idx] is None else acc[idx] + a
            for idx in range(2 if s < 2 else 4):
                a_scr[(2 * s if s < 2 else 4) + idx] = acc[idx].astype(_BF16)

    def outputs(r0):
        for s in range(N_SLABS):
            q, k = load_qk(r0, s)
            cum = x_scrs[s][0:CHUNK, :]
            e_q = jnp.exp2(cum)
            e_k = jnp.exp2(cum[CHUNK - 1:CHUNK, :] - cum)
            qd = q * e_q.astype(_BF16)
            kd = k * e_k.astype(_BF16)
            chunk_decay = e_q[CHUNK - 1:CHUNK, :]
            for idx in range(2 if s < 2 else 4):
                if s < 2:
                    h = 2 * s + idx
                    js = slice(idx * LANE, (idx + 1) * LANE)
                    qd_h, kd_h = qd[:, js], kd[:, js]
                    v_col, gate_col = _HI + h * HEAD, _HGT + h * HEAD
                else:
                    h = 4 + idx
                    js = slice((idx // 2) * LANE, (idx // 2 + 1) * LANE)
                    qd_h, kd_h = qd[:, js] * half[idx % 2], kd[:, js] * half[idx % 2]
                    v_col, gate_col = _GV + idx * HEAD, _GG + idx * HEAD
                hs = slice(h * HEAD, (h + 1) * HEAD)
                st = st_scr[h]
                v_h = pb(r0, v_col, HEAD)
                o_h = _dot(a_scr[h], v_h) + _dot_nt(qd_h, st.astype(_BF16))
                v_t = v_h.astype(_F32).T.astype(_BF16)
                st_scr[h] = st * chunk_decay[:, js] + _dot(v_t, kd_h)
                o_h = o_h * lax.rsqrt(jnp.mean(o_h * o_h, axis=-1, keepdims=True) + EPS)
                gate = pb(r0, gate_col, HEAD).astype(_F32)
                o_h = o_h * hnw_ref[:, hs] * (gate * _sigmoid(gate))
                o_scr[r0:r0 + CHUNK, hs] = o_h.astype(_BF16)

    pending = list(range(n_proj_tiles))
    n_chunks = MIXER_BLOCK // CHUNK
    per_slot = -(-n_proj_tiles // (2 * n_chunks))
    for ci in range(n_chunks):
        r0 = ci * CHUNK
        decay_sums(r0)
        for _ in range(per_slot):
            if pending:
                project_tile(xn_next, pending.pop(0), nxt)
        intra_scores(r0)
        for _ in range(per_slot):
            if pending:
                project_tile(xn_next, pending.pop(0), nxt)
        outputs(r0)
    assert not pending

    o_ref[0] = xc_ref[0] + _dot(o_scr[...], wout_ref[...])


def _ffn_kernel(h_ref, nw_ref, wup_ref, cw_ref, cb_ref, wdn_ref, fw_ref, o_ref,
                a_ext, u_scr):
    tb = h_ref.shape[1]

    @pl.when(pl.program_id(1) == 0)
    def _():
        a_ext[0:HALO, :] = jnp.zeros((HALO, D_FF), _F32)

    @pl.when(pl.program_id(1) > 0)
    def _():
        a_ext[0:HALO, :] = a_ext[tb:tb + HALO, :]

    h = h_ref[0]
    hn = h * lax.rsqrt(jnp.mean(h * h, axis=-1, keepdims=True) + EPS) * nw_ref[...]
    hn = hn.astype(_BF16)
    for c0 in range(0, D_FF, FF_TILE):
        cs = slice(c0, c0 + FF_TILE)
        a_ext[HALO:HALO + tb, cs] = _dot(hn, wup_ref[:, cs])
        gate = _dot(hn, wup_ref[:, D_FF + c0:D_FF + c0 + FF_TILE])
        u = (cb_ref[:, cs]
             + a_ext[HALO - 2:HALO - 2 + tb, cs] * cw_ref[0:1, cs]
             + a_ext[HALO - 1:HALO - 1 + tb, cs] * cw_ref[1:2, cs]
             + a_ext[HALO:HALO + tb, cs] * cw_ref[2:3, cs])
        u_scr[:, cs] = (u * _sigmoid(u) * gate).astype(_BF16)
    y = h + _dot(u_scr[...], wdn_ref[...])
    o_ref[0] = y * lax.rsqrt(jnp.mean(y * y, axis=-1, keepdims=True) + EPS) * fw_ref[...]


def _const_spec(shape):
    zeros = (0,) * len(shape)
    return pl.BlockSpec(shape, lambda *_: zeros, pipeline_mode=pl.Buffered(1))


def kernel(x, attn_norm_w, w_in, hgrn_lower_bounds, w_gla_gate_up, b_gla_gate, hg_norm_w,
           gla_norm_w, w_out, ffn_norm_w, w_up, conv_w, conv_b, w_down, final_norm_w):
    batch, seq, d = x.shape
    assert d == D_MODEL and seq % MIXER_BLOCK == 0 and seq % FFN_BLOCK == 0
    assert w_in.shape[0] == 1, "single-layer problem"

    assert w_in.shape[2] == _GD + GLA_GATE_RANK
    win_t = jnp.transpose(w_in[0])
    wgu = w_gla_gate_up[0]
    wgu_hi = wgu.astype(_BF16)
    wgu_lo = (wgu - wgu_hi.astype(_F32)).astype(_BF16)
    head_nw = jnp.concatenate([hg_norm_w[0], gla_norm_w[0]])[None, :]
    n2_np, mask_np = _level_matrices(CHUNK)
    n2 = jnp.asarray(n2_np, dtype=_BF16)
    masks = jnp.asarray(mask_np)

    steps_per_row = seq // MIXER_BLOCK
    n_steps = batch * steps_per_row
    xb = x.reshape(n_steps, MIXER_BLOCK, D_MODEL)
    blk = (1, MIXER_BLOCK, D_MODEL)
    h = pl.pallas_call(
        functools.partial(_mixer_kernel, steps_per_row),
        grid=(n_steps,),
        in_specs=[
            pl.BlockSpec(blk, lambda i: (i, 0, 0)),
            pl.BlockSpec(blk, lambda i: (jnp.minimum(i + 1, n_steps - 1), 0, 0)),
            _const_spec((1, D_MODEL)),
            _const_spec(win_t.shape),
            _const_spec(hgrn_lower_bounds.shape),
            _const_spec((GLA_GATE_RANK, GLA_KEY_WIDTH)),
            _const_spec((GLA_GATE_RANK, GLA_KEY_WIDTH)),
            _const_spec((1, GLA_KEY_WIDTH)),
            _const_spec((1, MIX)),
            _const_spec((MIX, D_MODEL)),
            _const_spec(n2.shape),
            _const_spec(masks.shape),
        ],
        out_specs=pl.BlockSpec(blk, lambda i: (i, 0, 0)),
        out_shape=jax.ShapeDtypeStruct(xb.shape, _F32),
        scratch_shapes=[
            pltpu.VMEM((2, MIXER_BLOCK, _PF_W), _F32),
            pltpu.VMEM((2, MIXER_BLOCK, _GD), _BF16),
            *[pltpu.VMEM((n2.shape[0], SLAB), _F32)
              for _ in range(N_SLABS)],
            pltpu.VMEM((CHUNK, HG_WIDTH), _BF16),
            pltpu.VMEM((N_HEADS, CHUNK, CHUNK), _BF16),
            pltpu.VMEM((N_HEADS, HEAD, LANE), _F32),
            pltpu.VMEM((MIXER_BLOCK, MIX), _BF16),
        ],
        compiler_params=pltpu.CompilerParams(dimension_semantics=("arbitrary",),
                                             vmem_limit_bytes=VMEM_LIMIT),
        name="mixer",
    )(xb, xb, attn_norm_w, win_t, hgrn_lower_bounds, wgu_hi, wgu_lo, b_gla_gate, head_nw,
      w_out[0].astype(_BF16), n2, masks).reshape(x.shape)

    def act_spec(block):
        return pl.BlockSpec((1, block, D_MODEL), lambda b, t: (b, t, 0))

    out = pl.pallas_call(
        _ffn_kernel,
        grid=(batch, seq // FFN_BLOCK),
        in_specs=[
            act_spec(FFN_BLOCK),
            _const_spec((1, D_MODEL)),
            _const_spec((D_MODEL, 2 * D_FF)),
            _const_spec((3, D_FF)),
            _const_spec((1, D_FF)),
            _const_spec((D_FF, D_MODEL)),
            _const_spec((1, D_MODEL)),
        ],
        out_specs=act_spec(FFN_BLOCK),
        out_shape=jax.ShapeDtypeStruct(x.shape, _F32),
        scratch_shapes=[
            pltpu.VMEM((HALO + FFN_BLOCK, D_FF), _F32),
            pltpu.VMEM((FFN_BLOCK, D_FF), _BF16),
        ],
        compiler_params=pltpu.CompilerParams(dimension_semantics=("arbitrary", "arbitrary"),
                                             vmem_limit_bytes=VMEM_LIMIT),
        name="ffn",
    )(h, ffn_norm_w, w_up[0].astype(_BF16), conv_w[0], conv_b, w_down[0].astype(_BF16),
      final_norm_w[None, :])
    return out
```

```python
import numpy as np
import jax
import jax.numpy as jnp
from jax import lax
from jax.experimental import pallas as pl
from jax.experimental.pallas import tpu as pltpu

D_MODEL = 1024
HG_WIDTH = 512
GLA_WIDTH = 512
GLA_DK = 64
GLA_KEY_WIDTH = 256
GLA_GATE_RANK = 16
GLA_GATE_TAU = 16.0
D_FF = 2816
EPS = 1e-6
LOG2_E = 1.4426950408889634

LANE = 128
HEAD = 128
N_HEADS = 8
MIX = N_HEADS * HEAD
SLAB = 2 * LANE
N_SLABS = 3
CHUNK = 128
MIXER_BLOCK = 512
FFN_BLOCK = 1024
FF_TILE = 256
SUBLANES = 8
HALO = SUBLANES
VMEM_LIMIT = 58 * 1024 * 1024

_HQ, _HF, _HI, _HGT = 0, 512, 1024, 1536
_GQ, _GK, _GV, _GG, _GD = 2048, 2304, 2560, 3072, 3584
PROJ_W = _GD + LANE

_F32 = jnp.float32
_BF16 = jnp.bfloat16


def _level_matrices(c):
    r = np.arange(c)[:, None]
    s = np.arange(c)[None, :]
    blocks = [s <= r]
    masks = [np.eye(c, dtype=bool)]
    m = 1
    while m < c:
        ref = (r // (2 * m)) * (2 * m) + m - 1
        hi = (r // m) % 2 == 1
        if m < SUBLANES:
            blocks.append(np.where(hi, (s > ref) & (s <= r), (s > r) & (s <= ref)))
        masks.append(((r // (2 * m)) == (s // (2 * m))) & hi & ((s // m) % 2 == 0))
        m *= 2
    n = np.concatenate(blocks, axis=0).astype(np.float32)
    n2 = np.concatenate([n, n], axis=-1)
    return n2, np.stack(masks).astype(np.float32)


def _sigmoid(x):
    return 1.0 / (1.0 + jnp.exp(-x))


def _split_bf16(x):
    hi = x.astype(_BF16)
    lo = (x - hi.astype(_F32)).astype(_BF16)
    return hi, lo


def _dot(a, b):
    return jnp.dot(a, b, preferred_element_type=_F32)


def _dot_nt(a, b):
    return lax.dot_general(a, b, (((1,), (1,)), ((), ())), preferred_element_type=_F32)


def _mixer_kernel(x_ref, nw_ref, win_ref, lb_ref, wgh_ref, wgl_ref, bg_ref, hnw_ref,
                  wout_ref, n2_ref, mask_ref, o_ref,
                  proj_scr, xs0, xs1, xs2, k_scr, a_scr, st_scr, o_scr):
    n_levels = mask_ref.shape[0]
    x_scrs = (xs0, xs1, xs2)

    @pl.when(pl.program_id(1) == 0)
    def _():
        st_scr[...] = jnp.zeros_like(st_scr)

    x = x_ref[0]
    xn = x * lax.rsqrt(jnp.mean(x * x, axis=-1, keepdims=True) + EPS) * nw_ref[...]
    xn = xn.astype(_BF16)
    for c0 in range(0, _GD, 512):
        proj_scr[:, c0:c0 + 512] = _dot_nt(xn, win_ref[c0:c0 + 512, :].astype(_BF16))
    proj_scr[:, _GD:_GD + GLA_GATE_RANK] = _dot_nt(
        xn, win_ref[_GD:_GD + GLA_GATE_RANK, :].astype(_BF16))

    p = lb_ref[...]
    pe = jnp.exp(p - jnp.max(p, axis=0, keepdims=True))
    lb = pe[0:1, :] / jnp.sum(pe, axis=0, keepdims=True)

    lane = lax.broadcasted_iota(jnp.int32, (1, LANE), 1)
    half = [(lane < GLA_DK).astype(_BF16), (lane >= GLA_DK).astype(_BF16)]

    def chunk_body(ci, carry):
        rows = pl.ds(pl.multiple_of(ci * CHUNK, CHUNK), CHUNK)

        def load_qk(s):
            if s < 2:
                c0 = s * SLAB
                return proj_scr[rows, _HQ + c0:_HQ + c0 + SLAB], k_scr[:, c0:c0 + SLAB]
            q = proj_scr[rows, _GQ:_GQ + SLAB] * (GLA_DK ** -0.5)
            return q, proj_scr[rows, _GK:_GK + SLAB]

        for s in range(N_SLABS):
            if s < 2:
                c0 = s * SLAB
                lb_s = lb[:, c0:c0 + SLAB]
                f = lb_s + (1.0 - lb_s) * _sigmoid(proj_scr[rows, _HF + c0:_HF + c0 + SLAB])
                g = jnp.log2(f)
                k_scr[:, c0:c0 + SLAB] = 1.0 - f
            else:
                gd_hi, gd_lo = _split_bf16(proj_scr[rows, _GD:_GD + GLA_GATE_RANK])
                logit = (_dot(gd_hi, wgh_ref[...]) + _dot(gd_lo, wgh_ref[...])
                         + _dot(gd_hi, wgl_ref[...]) + bg_ref[...])
                g = (jnp.minimum(logit, 0.0) * LOG2_E
                     - jnp.log2(1.0 + jnp.exp(-jnp.abs(logit)))) * (1.0 / GLA_GATE_TAU)
            g_hi, g_lo = _split_bf16(g)
            x_scrs[s][...] = _dot(n2_ref[...], jnp.concatenate([g_hi, g_lo], axis=0))

        def level_exponent(s, lv):
            m = 1 << (lv - 1)
            if m < SUBLANES:
                return x_scrs[s][lv * CHUNK:(lv + 1) * CHUNK, :]
            cum = x_scrs[s][0:CHUNK, :]
            ref = jnp.concatenate(
                [jnp.broadcast_to(cum[r:r + 1, :], (2 * m, SLAB))
                 for r in range(m - 1, CHUNK, 2 * m)], axis=0)
            return -jnp.abs(cum - ref)

        for s in range(N_SLABS):
            q, k = load_qk(s)
            q, k = q.astype(_BF16), k.astype(_BF16)
            acc = [None] * 4
            for lv in range(n_levels):
                if lv == 0:
                    qp, kp = q, k
                else:
                    e = jnp.exp2(level_exponent(s, lv)).astype(_BF16)
                    qp, kp = q * e, k * e
                m = mask_ref[lv]
                for j in range(2):
                    js = slice(j * LANE, (j + 1) * LANE)
                    if s < 2:
                        parts = [_dot_nt(qp[:, js], kp[:, js])]
                    else:
                        both = jnp.concatenate([qp[:, js] * half[0], qp[:, js] * half[1]], axis=0)
                        r = _dot_nt(both, kp[:, js])
                        parts = [r[0:CHUNK], r[CHUNK:2 * CHUNK]]
                    for i, part in enumerate(parts):
                        idx = j * len(parts) + i
                        a = part * m
                        acc[idx] = a if acc[idx] is None else acc[idx] + a
            for idx in range(2 if s < 2 else 4):
                a_scr[(2 * s if s < 2 else 4) + idx] = acc[idx].astype(_BF16)

        for s in range(N_SLABS):
            q, k = load_qk(s)
            cum = x_scrs[s][0:CHUNK, :]
            e_q = jnp.exp2(cum)
            e_k = jnp.exp2(cum[CHUNK - 1:CHUNK, :] - cum)
            qd = (q * e_q).astype(_BF16)
            kd = (k * e_k).astype(_BF16)
            chunk_decay = e_q[CHUNK - 1:CHUNK, :]
            for idx in range(2 if s < 2 else 4):
                if s < 2:
                    h = 2 * s + idx
                    js = slice(idx * LANE, (idx + 1) * LANE)
                    qd_h, kd_h = qd[:, js], kd[:, js]
                    v_col, gate_col = _HI + h * HEAD, _HGT + h * HEAD
                else:
                    h = 4 + idx
                    js = slice((idx // 2) * LANE, (idx // 2 + 1) * LANE)
                    qd_h, kd_h = qd[:, js] * half[idx % 2], kd[:, js] * half[idx % 2]
                    v_col, gate_col = _GV + idx * HEAD, _GG + idx * HEAD
                hs = slice(h * HEAD, (h + 1) * HEAD)
                st = st_scr[h]
                v_f = proj_scr[rows, v_col:v_col + HEAD]
                o_h = _dot(a_scr[h], v_f.astype(_BF16)) + _dot_nt(qd_h, st.astype(_BF16))
                st_scr[h] = st * chunk_decay[:, js] + _dot(v_f.T.astype(_BF16), kd_h)
                o_h = o_h * lax.rsqrt(jnp.mean(o_h * o_h, axis=-1, keepdims=True) + EPS)
                gate = proj_scr[rows, gate_col:gate_col + HEAD]
                o_h = o_h * hnw_ref[:, hs] * (gate * _sigmoid(gate))
                o_scr[rows, hs] = o_h.astype(_BF16)
        return carry

    lax.fori_loop(0, MIXER_BLOCK // CHUNK, chunk_body, 0, unroll=True)

    o_ref[0] = x + _dot(o_scr[...], wout_ref[...])


def _ffn_kernel(h_ref, nw_ref, wup_ref, cw_ref, cb_ref, wdn_ref, fw_ref, o_ref,
                a_ext, u_scr):
    tb = h_ref.shape[1]

    @pl.when(pl.program_id(1) == 0)
    def _():
        a_ext[0:HALO, :] = jnp.zeros((HALO, D_FF), _F32)

    @pl.when(pl.program_id(1) > 0)
    def _():
        a_ext[0:HALO, :] = a_ext[tb:tb + HALO, :]

    h = h_ref[0]
    hn = h * lax.rsqrt(jnp.mean(h * h, axis=-1, keepdims=True) + EPS) * nw_ref[...]
    hn = hn.astype(_BF16)
    for c0 in range(0, D_FF, FF_TILE):
        cs = slice(c0, c0 + FF_TILE)
        a_ext[HALO:HALO + tb, cs] = _dot(hn, wup_ref[:, cs])
        gate = _dot(hn, wup_ref[:, D_FF + c0:D_FF + c0 + FF_TILE])
        u = (cb_ref[:, cs]
             + a_ext[HALO - 2:HALO - 2 + tb, cs] * cw_ref[0:1, cs]
             + a_ext[HALO - 1:HALO - 1 + tb, cs] * cw_ref[1:2, cs]
             + a_ext[HALO:HALO + tb, cs] * cw_ref[2:3, cs])
        u_scr[:, cs] = (u * _sigmoid(u) * gate).astype(_BF16)
    y = h + _dot(u_scr[...], wdn_ref[...])
    o_ref[0] = y * lax.rsqrt(jnp.mean(y * y, axis=-1, keepdims=True) + EPS) * fw_ref[...]


def _const_spec(shape):
    zeros = (0,) * len(shape)
    return pl.BlockSpec(shape, lambda b, t: zeros, pipeline_mode=pl.Buffered(1))


def kernel(x, attn_norm_w, w_in, hgrn_lower_bounds, w_gla_gate_up, b_gla_gate, hg_norm_w,
           gla_norm_w, w_out, ffn_norm_w, w_up, conv_w, conv_b, w_down, final_norm_w):
    batch, seq, d = x.shape
    assert d == D_MODEL and seq % MIXER_BLOCK == 0 and seq % FFN_BLOCK == 0
    assert w_in.shape[0] == 1, "single-layer problem"

    assert w_in.shape[2] == _GD + GLA_GATE_RANK
    win_t = jnp.transpose(w_in[0])
    wgu = w_gla_gate_up[0]
    wgu_hi = wgu.astype(_BF16)
    wgu_lo = (wgu - wgu_hi.astype(_F32)).astype(_BF16)
    head_nw = jnp.concatenate([hg_norm_w[0], gla_norm_w[0]])[None, :]
    n2_np, mask_np = _level_matrices(CHUNK)
    n2 = jnp.asarray(n2_np, dtype=_BF16)
    masks = jnp.asarray(mask_np)

    params = pltpu.CompilerParams(dimension_semantics=("arbitrary", "arbitrary"),
                                  vmem_limit_bytes=VMEM_LIMIT)

    def act_spec(block):
        return pl.BlockSpec((1, block, D_MODEL), lambda b, t: (b, t, 0))

    h = pl.pallas_call(
        _mixer_kernel,
        grid=(batch, seq // MIXER_BLOCK),
        in_specs=[
            act_spec(MIXER_BLOCK),
            _const_spec((1, D_MODEL)),
            _const_spec(win_t.shape),
            _const_spec(hgrn_lower_bounds.shape),
            _const_spec((GLA_GATE_RANK, GLA_KEY_WIDTH)),
            _const_spec((GLA_GATE_RANK, GLA_KEY_WIDTH)),
            _const_spec((1, GLA_KEY_WIDTH)),
            _const_spec((1, MIX)),
            _const_spec((MIX, D_MODEL)),
            _const_spec(n2.shape),
            _const_spec(masks.shape),
        ],
        out_specs=act_spec(MIXER_BLOCK),
        out_shape=jax.ShapeDtypeStruct(x.shape, _F32),
        scratch_shapes=[
            pltpu.VMEM((MIXER_BLOCK, PROJ_W), _F32),
            *[pltpu.VMEM((n2.shape[0], SLAB), _F32)
              for _ in range(N_SLABS)],
            pltpu.VMEM((CHUNK, HG_WIDTH), _F32),
            pltpu.VMEM((N_HEADS, CHUNK, CHUNK), _BF16),
            pltpu.VMEM((N_HEADS, HEAD, LANE), _F32),
            pltpu.VMEM((MIXER_BLOCK, MIX), _BF16),
        ],
        compiler_params=params,
        name="mixer",
    )(x, attn_norm_w, win_t, hgrn_lower_bounds, wgu_hi, wgu_lo, b_gla_gate, head_nw,
      w_out[0].astype(_BF16), n2, masks)

    out = pl.pallas_call(
        _ffn_kernel,
        grid=(batch, seq // FFN_BLOCK),
        in_specs=[
            act_spec(FFN_BLOCK),
            _const_spec((1, D_MODEL)),
            _const_spec((D_MODEL, 2 * D_FF)),
            _const_spec((3, D_FF)),
            _const_spec((1, D_FF)),
            _const_spec((D_FF, D_MODEL)),
            _const_spec((1, D_MODEL)),
        ],
        out_specs=act_spec(FFN_BLOCK),
        out_shape=jax.ShapeDtypeStruct(x.shape, _F32),
        scratch_shapes=[
            pltpu.VMEM((HALO + FFN_BLOCK, D_FF), _F32),
            pltpu.VMEM((FFN_BLOCK, D_FF), _BF16),
        ],
        compiler_params=params,
        name="ffn",
    )(h, ffn_norm_w, w_up[0].astype(_BF16), conv_w[0], conv_b, w_down[0].astype(_BF16),
      final_norm_w[None, :])
    return out
```

```python
import numpy as np
import jax
import jax.numpy as jnp
from jax import lax
from jax.experimental import pallas as pl
from jax.experimental.pallas import tpu as pltpu

D_MODEL = 1024
HG_WIDTH = 512
GLA_WIDTH = 512
GLA_DK = 64
GLA_KEY_WIDTH = 256
GLA_GATE_RANK = 16
GLA_GATE_TAU = 16.0
D_FF = 2816
EPS = 1e-6
LOG2_E = 1.4426950408889634

LANE = 128
HEAD = 128
N_HEADS = 8
MIX = N_HEADS * HEAD
SLAB = 2 * LANE
N_SLABS = 3
CHUNK = 128
MIXER_BLOCK = 512
FFN_BLOCK = 1024
FF_TILE = 256
WDN_SLAB = 128
SUBLANES = 8
HALO = SUBLANES
VMEM_LIMIT = 58 * 1024 * 1024

_HQ, _HF, _HI, _HGT = 0, 512, 1024, 1536
_GQ, _GK, _GV, _GG, _GD = 2048, 2304, 2560, 3072, 3584
PROJ_W = _GD + LANE

_F32 = jnp.float32
_BF16 = jnp.bfloat16


def _level_matrices(c):
    r = np.arange(c)[:, None]
    s = np.arange(c)[None, :]
    blocks = [s <= r]
    masks = [np.eye(c, dtype=bool)]
    m = 1
    while m < c:
        ref = (r // (2 * m)) * (2 * m) + m - 1
        hi = (r // m) % 2 == 1
        if 1 < m < SUBLANES:
            blocks.append(np.where(hi, (s > ref) & (s <= r), (s > r) & (s <= ref)))
        masks.append(((r // (2 * m)) == (s // (2 * m))) & hi & ((s // m) % 2 == 0))
        m *= 2
    n = np.concatenate(blocks, axis=0).astype(np.float32)
    n2 = np.concatenate([n, n], axis=-1)
    return n2, np.stack(masks).astype(np.float32)


def _sigmoid(x):
    return 1.0 / (1.0 + jnp.exp(-x))


def _split_bf16(x):
    hi = x.astype(_BF16)
    lo = (x - hi.astype(_F32)).astype(_BF16)
    return hi, lo


def _dot(a, b):
    return jnp.dot(a, b, preferred_element_type=_F32)


def _dot_nt(a, b):
    return lax.dot_general(a, b, (((1,), (1,)), ((), ())), preferred_element_type=_F32)


def _mixer_kernel(x_ref, nw_ref, win_ref, lb_ref, wgh_ref, wgl_ref, bg_ref, hnw_ref,
                  wout_ref, n2_ref, mask_ref, wup_ref, wdn_ref, o_ref, wup_o_ref, wdn_o_ref,
                  proj_scr, xs0, xs1, xs2, k_scr, a_scr, st_scr, o_scr):
    n_levels = mask_ref.shape[0]

    wup_o_ref[...] = wup_ref[0].astype(_BF16)
    wdn_o_ref[...] = wdn_ref[0].astype(_BF16)
    x_scrs = (xs0, xs1, xs2)

    @pl.when(pl.program_id(1) == 0)
    def _():
        st_scr[...] = jnp.zeros_like(st_scr)

    x = x_ref[0]
    xn = x * lax.rsqrt(jnp.mean(x * x, axis=-1, keepdims=True) + EPS) * nw_ref[...]
    xn = xn.astype(_BF16)
    for c0 in range(0, _GD, 512):
        proj_scr[:, c0:c0 + 512] = _dot_nt(xn, win_ref[c0:c0 + 512, :].astype(_BF16))
    proj_scr[:, _GD:_GD + GLA_GATE_RANK] = _dot_nt(
        xn, win_ref[_GD:_GD + GLA_GATE_RANK, :].astype(_BF16))

    p = lb_ref[...]
    pe = jnp.exp(p - jnp.max(p, axis=0, keepdims=True))
    lb = pe[0:1, :] / jnp.sum(pe, axis=0, keepdims=True)

    lane = lax.broadcasted_iota(jnp.int32, (1, LANE), 1)
    half = [(lane < GLA_DK).astype(_BF16), (lane >= GLA_DK).astype(_BF16)]
    odd_row = (lax.broadcasted_iota(jnp.int32, (CHUNK, SLAB), 0) & 1) == 1
    n_mxu_rows = n2_ref.shape[0]

    def chunk_body(ci, carry):
        rows = pl.ds(pl.multiple_of(ci * CHUNK, CHUNK), CHUNK)

        def load_qk(s):
            if s < 2:
                c0 = s * SLAB
                return proj_scr[rows, _HQ + c0:_HQ + c0 + SLAB], k_scr[:, c0:c0 + SLAB]
            q = proj_scr[rows, _GQ:_GQ + SLAB] * (GLA_DK ** -0.5)
            return q, proj_scr[rows, _GK:_GK + SLAB]

        for s in range(N_SLABS):
            if s < 2:
                c0 = s * SLAB
                lb_s = lb[:, c0:c0 + SLAB]
                f = lb_s + (1.0 - lb_s) * _sigmoid(proj_scr[rows, _HF + c0:_HF + c0 + SLAB])
                g = jnp.log2(f)
                k_scr[:, c0:c0 + SLAB] = 1.0 - f
            else:
                gd_hi, gd_lo = _split_bf16(proj_scr[rows, _GD:_GD + GLA_GATE_RANK])
                logit = (_dot(gd_hi, wgh_ref[...]) + _dot(gd_lo, wgh_ref[...])
                         + _dot(gd_hi, wgl_ref[...]) + bg_ref[...])
                g = (jnp.minimum(logit, 0.0) * LOG2_E
                     - jnp.log2(1.0 + jnp.exp(-jnp.abs(logit)))) * (1.0 / GLA_GATE_TAU)
            g_hi, g_lo = _split_bf16(g)
            x_scrs[s][0:n_mxu_rows, :] = _dot(n2_ref[...], jnp.concatenate([g_hi, g_lo], axis=0))
            x_scrs[s][n_mxu_rows:n_mxu_rows + CHUNK, :] = g

        def level_exponent(s, lv):
            m = 1 << (lv - 1)
            if m == 1:
                return jnp.where(odd_row, x_scrs[s][n_mxu_rows:n_mxu_rows + CHUNK, :], 0.0)
            if m < SUBLANES:
                return x_scrs[s][(lv - 1) * CHUNK:lv * CHUNK, :]
            cum = x_scrs[s][0:CHUNK, :]
            ref = jnp.concatenate(
                [jnp.broadcast_to(cum[r:r + 1, :], (2 * m, SLAB))
                 for r in range(m - 1, CHUNK, 2 * m)], axis=0)
            return -jnp.abs(cum - ref)

        for s in range(N_SLABS):
            q, k = load_qk(s)
            q, k = q.astype(_BF16), k.astype(_BF16)
            acc = [None] * 4
            for lv in range(n_levels):
                if lv == 0:
                    qp, kp = q, k
                else:
                    e = jnp.exp2(level_exponent(s, lv)).astype(_BF16)
                    qp, kp = q * e, k * e
                m = mask_ref[lv]
                for j in range(2):
                    js = slice(j * LANE, (j + 1) * LANE)
                    if s < 2:
                        parts = [_dot_nt(qp[:, js], kp[:, js])]
                    else:
                        both = jnp.concatenate([qp[:, js] * half[0], qp[:, js] * half[1]], axis=0)
                        r = _dot_nt(both, kp[:, js])
                        parts = [r[0:CHUNK], r[CHUNK:2 * CHUNK]]
                    for i, part in enumerate(parts):
                        idx = j * len(parts) + i
                        a = part * m
                        acc[idx] = a if acc[idx] is None else acc[idx] + a
            for idx in range(2 if s < 2 else 4):
                a_scr[(2 * s if s < 2 else 4) + idx] = acc[idx].astype(_BF16)

        for s in range(N_SLABS):
            q, k = load_qk(s)
            cum = x_scrs[s][0:CHUNK, :]
            e_q = jnp.exp2(cum)
            e_k = jnp.exp2(cum[CHUNK - 1:CHUNK, :] - cum)
            qd = (q * e_q).astype(_BF16)
            kd = (k * e_k).astype(_BF16)
            chunk_decay = e_q[CHUNK - 1:CHUNK, :]
            for idx in range(2 if s < 2 else 4):
                if s < 2:
                    h = 2 * s + idx
                    js = slice(idx * LANE, (idx + 1) * LANE)
                    qd_h, kd_h = qd[:, js], kd[:, js]
                    v_col, gate_col = _HI + h * HEAD, _HGT + h * HEAD
                else:
                    h = 4 + idx
                    js = slice((idx // 2) * LANE, (idx // 2 + 1) * LANE)
                    qd_h, kd_h = qd[:, js] * half[idx % 2], kd[:, js] * half[idx % 2]
                    v_col, gate_col = _GV + idx * HEAD, _GG + idx * HEAD
                hs = slice(h * HEAD, (h + 1) * HEAD)
                st = st_scr[h]
                v_f = proj_scr[rows, v_col:v_col + HEAD]
                o_h = _dot(a_scr[h], v_f.astype(_BF16)) + _dot_nt(qd_h, st.astype(_BF16))
                st_scr[h] = st * chunk_decay[:, js] + _dot(v_f.T.astype(_BF16), kd_h)
                o_h = o_h * lax.rsqrt(jnp.mean(o_h * o_h, axis=-1, keepdims=True) + EPS)
                gate = proj_scr[rows, gate_col:gate_col + HEAD]
                o_h = o_h * hnw_ref[:, hs] * (gate * _sigmoid(gate))
                o_scr[rows, hs] = o_h.astype(_BF16)
        return carry

    lax.fori_loop(0, MIXER_BLOCK // CHUNK, chunk_body, 0, unroll=True)

    o_ref[0] = x + _dot(o_scr[...], wout_ref[0].astype(_BF16))


def _ffn_kernel(h_ref, nw_ref, wup_ref, cw_ref, cb_ref, wdn_ref, fw_ref, o_ref,
                a_ext, u_scr):
    tb = h_ref.shape[1]

    @pl.when(pl.program_id(1) == 0)
    def _():
        a_ext[0:HALO, :] = jnp.zeros((HALO, D_FF), _F32)

    @pl.when(pl.program_id(1) > 0)
    def _():
        a_ext[0:HALO, :] = a_ext[tb:tb + HALO, :]

    h = h_ref[0]
    hn = h * lax.rsqrt(jnp.mean(h * h, axis=-1, keepdims=True) + EPS) * nw_ref[...]
    hn = hn.astype(_BF16)
    for c0 in range(0, D_FF, FF_TILE):
        cs = slice(c0, c0 + FF_TILE)
        a_ext[HALO:HALO + tb, cs] = _dot(hn, wup_ref[:, cs])
        gate = _dot(hn, wup_ref[:, D_FF + c0:D_FF + c0 + FF_TILE])
        u = (cb_ref[:, cs]
             + a_ext[HALO - 2:HALO - 2 + tb, cs] * cw_ref[0:1, cs]
             + a_ext[HALO - 1:HALO - 1 + tb, cs] * cw_ref[1:2, cs]
             + a_ext[HALO:HALO + tb, cs] * cw_ref[2:3, cs])
        u_scr[:, cs] = (u * _sigmoid(u) * gate).astype(_BF16)
    y = h + _dot(u_scr[...], wdn_ref[...])
    o_ref[0] = y * lax.rsqrt(jnp.mean(y * y, axis=-1, keepdims=True) + EPS) * fw_ref[...]


def _const_spec(shape):
    zeros = (0,) * len(shape)
    return pl.BlockSpec(shape, lambda b, t: zeros, pipeline_mode=pl.Buffered(1))


def kernel(x, attn_norm_w, w_in, hgrn_lower_bounds, w_gla_gate_up, b_gla_gate, hg_norm_w,
           gla_norm_w, w_out, ffn_norm_w, w_up, conv_w, conv_b, w_down, final_norm_w):
    batch, seq, d = x.shape
    assert d == D_MODEL and seq % MIXER_BLOCK == 0 and seq % FFN_BLOCK == 0
    assert w_in.shape[0] == 1, "single-layer problem"

    assert w_in.shape[2] == _GD + GLA_GATE_RANK
    win_t = jnp.transpose(w_in[0])
    wgu = w_gla_gate_up[0]
    wgu_hi = wgu.astype(_BF16)
    wgu_lo = (wgu - wgu_hi.astype(_F32)).astype(_BF16)
    head_nw = jnp.concatenate([hg_norm_w[0], gla_norm_w[0]])[None, :]
    n2_np, mask_np = _level_matrices(CHUNK)
    n2 = jnp.asarray(n2_np, dtype=_BF16)
    masks = jnp.asarray(mask_np)

    params = pltpu.CompilerParams(dimension_semantics=("arbitrary", "arbitrary"),
                                  vmem_limit_bytes=VMEM_LIMIT)

    def act_spec(block):
        return pl.BlockSpec((1, block, D_MODEL), lambda b, t: (b, t, 0))

    steps_per_row = seq // MIXER_BLOCK
    n_steps = batch * steps_per_row
    assert D_MODEL % n_steps == 0 and D_FF % WDN_SLAB == 0 and D_FF // WDN_SLAB <= n_steps
    wup_rows = D_MODEL // n_steps
    last_wdn = D_FF // WDN_SLAB - 1

    def step(b, t):
        return b * steps_per_row + t

    h, w_up_bf16, w_down_bf16 = pl.pallas_call(
        _mixer_kernel,
        grid=(batch, steps_per_row),
        in_specs=[
            act_spec(MIXER_BLOCK),
            _const_spec((1, D_MODEL)),
            _const_spec(win_t.shape),
            _const_spec(hgrn_lower_bounds.shape),
            _const_spec((GLA_GATE_RANK, GLA_KEY_WIDTH)),
            _const_spec((GLA_GATE_RANK, GLA_KEY_WIDTH)),
            _const_spec((1, GLA_KEY_WIDTH)),
            _const_spec((1, MIX)),
            _const_spec(w_out.shape),
            _const_spec(n2.shape),
            _const_spec(masks.shape),
            pl.BlockSpec((1, wup_rows, 2 * D_FF), lambda b, t: (0, step(b, t), 0)),
            pl.BlockSpec((1, WDN_SLAB, D_MODEL),
                         lambda b, t: (0, jnp.minimum(step(b, t), last_wdn), 0)),
        ],
        out_specs=[
            act_spec(MIXER_BLOCK),
            pl.BlockSpec((wup_rows, 2 * D_FF), lambda b, t: (step(b, t), 0)),
            pl.BlockSpec((WDN_SLAB, D_MODEL), lambda b, t: (jnp.minimum(step(b, t), last_wdn), 0)),
        ],
        out_shape=[
            jax.ShapeDtypeStruct(x.shape, _F32),
            jax.ShapeDtypeStruct((D_MODEL, 2 * D_FF), _BF16),
            jax.ShapeDtypeStruct((D_FF, D_MODEL), _BF16),
        ],
        scratch_shapes=[
            pltpu.VMEM((MIXER_BLOCK, PROJ_W), _F32),
            *[pltpu.VMEM((n2.shape[0] + CHUNK, SLAB), _F32)
              for _ in range(N_SLABS)],
            pltpu.VMEM((CHUNK, HG_WIDTH), _F32),
            pltpu.VMEM((N_HEADS, CHUNK, CHUNK), _BF16),
            pltpu.VMEM((N_HEADS, HEAD, LANE), _F32),
            pltpu.VMEM((MIXER_BLOCK, MIX), _BF16),
        ],
        compiler_params=params,
        name="mixer",
    )(x, attn_norm_w, win_t, hgrn_lower_bounds, wgu_hi, wgu_lo, b_gla_gate, head_nw,
      w_out, n2, masks, w_up, w_down)

    out = pl.pallas_call(
        _ffn_kernel,
        grid=(batch, seq // FFN_BLOCK),
        in_specs=[
            act_spec(FFN_BLOCK),
            _const_spec((1, D_MODEL)),
            _const_spec((D_MODEL, 2 * D_FF)),
            _const_spec((3, D_FF)),
            _const_spec((1, D_FF)),
            _const_spec((D_FF, D_MODEL)),
            _const_spec((1, D_MODEL)),
        ],
        out_specs=act_spec(FFN_BLOCK),
        out_shape=jax.ShapeDtypeStruct(x.shape, _F32),
        scratch_shapes=[
            pltpu.VMEM((HALO + FFN_BLOCK, D_FF), _F32),
            pltpu.VMEM((FFN_BLOCK, D_FF), _BF16),
        ],
        compiler_params=params,
        name="ffn",
    )(h, ffn_norm_w, w_up_bf16, conv_w[0], conv_b, w_down_bf16,
      final_norm_w[None, :])
    return out
```

```python
import numpy as np
import jax
import jax.numpy as jnp
from jax import lax
from jax.experimental import pallas as pl
from jax.experimental.pallas import tpu as pltpu

D_MODEL = 1024
HG_WIDTH = 512
GLA_WIDTH = 512
GLA_DK = 64
GLA_KEY_WIDTH = 256
GLA_GATE_RANK = 16
GLA_GATE_TAU = 16.0
D_FF = 2816
EPS = 1e-6
LOG2_E = 1.4426950408889634

LANE = 128
HEAD = 128
N_HEADS = 8
MIX = N_HEADS * HEAD
SLAB = 2 * LANE
N_SLABS = 3
CHUNK = 128
MIXER_BLOCK = 512
FFN_BLOCK = 1024
FF_TILE = 256
WDN_SLAB = 128
SUBLANES = 8
BF16_ROWS = 2 * SUBLANES
HALO = SUBLANES
VMEM_LIMIT = 58 * 1024 * 1024

_HQ, _HF, _HI, _HGT = 0, 512, 1024, 1536
_GQ, _GK, _GV, _GG, _GD = 2048, 2304, 2560, 3072, 3584
PROJ_W = _GD + LANE

_F32 = jnp.float32
_BF16 = jnp.bfloat16


def _level_matrices(c):
    r = np.arange(c)[:, None]
    s = np.arange(c)[None, :]
    blocks = [s <= r]
    masks = [np.eye(c, dtype=bool)]
    m = 1
    while m < c:
        ref = (r // (2 * m)) * (2 * m) + m - 1
        hi = (r // m) % 2 == 1
        if 1 < m < SUBLANES:
            blocks.append(np.where(hi, (s > ref) & (s <= r), (s > r) & (s <= ref)))
        masks.append(((r // (2 * m)) == (s // (2 * m))) & hi & ((s // m) % 2 == 0))
        m *= 2
    n = np.concatenate(blocks, axis=0).astype(np.float32)
    n2 = np.concatenate([n, n], axis=-1)
    return n2, np.stack(masks).astype(np.float32)


def _sigmoid(x):
    return 1.0 / (1.0 + jnp.exp(-x))


def _split_bf16(x):
    hi = x.astype(_BF16)
    lo = (x - hi.astype(_F32)).astype(_BF16)
    return hi, lo


def _dot(a, b):
    return jnp.dot(a, b, preferred_element_type=_F32)


def _dot_nt(a, b):
    return lax.dot_general(a, b, (((1,), (1,)), ((), ())), preferred_element_type=_F32)


def _mixer_kernel(x_ref, nw_ref, win_ref, lb_ref, wgh_ref, wgl_ref, bg_ref, hnw_ref,
                  wout_ref, n2_ref, mask_ref, wup_ref, wdn_ref, o_ref, wup_o_ref, wdn_o_ref,
                  proj_scr, xs0, xs1, xs2, k_scr, a_scr, st_scr, o_scr):
    n_levels = mask_ref.shape[0]

    wup_o_ref[...] = wup_ref[0].astype(_BF16)
    wdn_o_ref[...] = wdn_ref[0].astype(_BF16)
    x_scrs = (xs0, xs1, xs2)

    @pl.when(pl.program_id(1) == 0)
    def _():
        st_scr[...] = jnp.zeros_like(st_scr)

    x = x_ref[0]
    xn = x * lax.rsqrt(jnp.mean(x * x, axis=-1, keepdims=True) + EPS) * nw_ref[...]
    xn = xn.astype(_BF16)
    for c0 in range(0, _GD, 512):
        proj_scr[:, c0:c0 + 512] = _dot_nt(xn, win_ref[c0:c0 + 512, :].astype(_BF16))
    proj_scr[:, _GD:_GD + GLA_GATE_RANK] = _dot_nt(
        xn, win_ref[_GD:_GD + GLA_GATE_RANK, :].astype(_BF16))

    p = lb_ref[...]
    pe = jnp.exp(p - jnp.max(p, axis=0, keepdims=True))
    lb = pe[0:1, :] / jnp.sum(pe, axis=0, keepdims=True)

    lane = lax.broadcasted_iota(jnp.int32, (1, LANE), 1)
    half_f = [(lane < GLA_DK).astype(_F32), (lane >= GLA_DK).astype(_F32)]
    half = [hf.astype(_BF16) for hf in half_f]
    odd_row = (lax.broadcasted_iota(jnp.int32, (CHUNK, SLAB), 0) & 1) == 1
    n_mxu_rows = n2_ref.shape[0]

    def chunk_body(ci, carry):
        rows = pl.ds(pl.multiple_of(ci * CHUNK, CHUNK), CHUNK)

        def load_qk(s):
            if s < 2:
                c0 = s * SLAB
                return proj_scr[rows, _HQ + c0:_HQ + c0 + SLAB], k_scr[:, c0:c0 + SLAB]
            q = proj_scr[rows, _GQ:_GQ + SLAB] * (GLA_DK ** -0.5)
            return q, proj_scr[rows, _GK:_GK + SLAB]

        for s in range(N_SLABS):
            if s < 2:
                c0 = s * SLAB
                lb_s = lb[:, c0:c0 + SLAB]
                f = lb_s + (1.0 - lb_s) * _sigmoid(proj_scr[rows, _HF + c0:_HF + c0 + SLAB])
                g = jnp.log2(f)
                k_scr[:, c0:c0 + SLAB] = 1.0 - f
            else:
                gd_hi, gd_lo = _split_bf16(proj_scr[rows, _GD:_GD + GLA_GATE_RANK])
                logit = (_dot(gd_hi, wgh_ref[...]) + _dot(gd_lo, wgh_ref[...])
                         + _dot(gd_hi, wgl_ref[...]) + bg_ref[...])
                g = (jnp.minimum(logit, 0.0) * LOG2_E
                     - jnp.log2(1.0 + jnp.exp(-jnp.abs(logit)))) * (1.0 / GLA_GATE_TAU)
            g_hi, g_lo = _split_bf16(g)
            x_scrs[s][0:n_mxu_rows, :] = _dot(n2_ref[...], jnp.concatenate([g_hi, g_lo], axis=0))
            x_scrs[s][n_mxu_rows:n_mxu_rows + CHUNK, :] = g

        def level_exponent(s, lv):
            m = 1 << (lv - 1)
            if m == 1:
                return jnp.where(odd_row, x_scrs[s][n_mxu_rows:n_mxu_rows + CHUNK, :], 0.0)
            if m < SUBLANES:
                return x_scrs[s][(lv - 1) * CHUNK:lv * CHUNK, :]
            cum = x_scrs[s][0:CHUNK, :]
            ref = jnp.concatenate(
                [jnp.broadcast_to(cum[r:r + 1, :], (2 * m, SLAB))
                 for r in range(m - 1, CHUNK, 2 * m)], axis=0)
            return -jnp.abs(cum - ref)

        for s in range(N_SLABS):
            q_f, k_f = load_qk(s)
            q, k = q_f.astype(_BF16), k_f.astype(_BF16)
            n_pieces = CHUNK // SUBLANES
            acc = [[None] * n_pieces for _ in range(4)]
            for lv in range(n_levels):
                if lv > 0:
                    e = jnp.exp2(level_exponent(s, lv)).astype(_BF16)
                    qp, kp = q * e, k * e
                blk = 1 << max(lv - 1, 0)
                if lv == 0 or blk < SUBLANES:
                    live = list(range(n_pieces))
                else:
                    live = [p for p in range(n_pieces) if (p * SUBLANES // blk) % 2 == 1]
                lhs_rows = live if blk >= BF16_ROWS else list(range(n_pieces))
                for j in range(2):
                    js = slice(j * LANE, (j + 1) * LANE)
                    if lv == 0:
                        prod = q_f[:, js] * k_f[:, js]
                        if s < 2:
                            parts = [jnp.sum(prod, axis=-1, keepdims=True)]
                        else:
                            parts = [jnp.sum(prod * half_f[i], axis=-1, keepdims=True)
                                     for i in range(2)]
                    else:
                        if len(lhs_rows) == n_pieces:
                            q_t = qp[:, js]
                        else:
                            q_t = jnp.concatenate(
                                [qp[r:r + blk, js] for r in range(blk, CHUNK, 2 * blk)], axis=0)
                        n_lhs = q_t.shape[0]
                        if s < 2:
                            parts = [_dot_nt(q_t, kp[:, js])]
                        else:
                            both = jnp.concatenate([q_t * half[0], q_t * half[1]], axis=0)
                            r = _dot_nt(both, kp[:, js])
                            parts = [r[0:n_lhs], r[n_lhs:2 * n_lhs]]
                    for i, part in enumerate(parts):
                        idx = j * len(parts) + i
                        for p in live:
                            src = lhs_rows.index(p) * SUBLANES
                            a = (part[src:src + SUBLANES]
                                 * mask_ref[lv, p * SUBLANES:(p + 1) * SUBLANES, :])
                            acc[idx][p] = a if acc[idx][p] is None else acc[idx][p] + a
            for idx in range(2 if s < 2 else 4):
                a_scr[(2 * s if s < 2 else 4) + idx] = jnp.concatenate(
                    acc[idx], axis=0).astype(_BF16)

        for s in range(N_SLABS):
            q, k = load_qk(s)
            cum = x_scrs[s][0:CHUNK, :]
            e_q = jnp.exp2(cum)
            e_k = jnp.exp2(cum[CHUNK - 1:CHUNK, :] - cum)
            qd = (q * e_q).astype(_BF16)
            kd = (k * e_k).astype(_BF16)
            chunk_decay = e_q[CHUNK - 1:CHUNK, :]
            for idx in range(2 if s < 2 else 4):
                if s < 2:
                    h = 2 * s + idx
                    js = slice(idx * LANE, (idx + 1) * LANE)
                    qd_h, kd_h = qd[:, js], kd[:, js]
                    v_col, gate_col = _HI + h * HEAD, _HGT + h * HEAD
                else:
                    h = 4 + idx
                    js = slice((idx // 2) * LANE, (idx // 2 + 1) * LANE)
                    qd_h, kd_h = qd[:, js] * half[idx % 2], kd[:, js] * half[idx % 2]
                    v_col, gate_col = _GV + idx * HEAD, _GG + idx * HEAD
                hs = slice(h * HEAD, (h + 1) * HEAD)
                st = st_scr[h]
                v_f = proj_scr[rows, v_col:v_col + HEAD]
                o_h = _dot(a_scr[h], v_f.astype(_BF16)) + _dot_nt(qd_h, st.astype(_BF16))
                st_scr[h] = st * chunk_decay[:, js] + _dot(v_f.T.astype(_BF16), kd_h)
                o_h = o_h * lax.rsqrt(jnp.mean(o_h * o_h, axis=-1, keepdims=True) + EPS)
                gate = proj_scr[rows, gate_col:gate_col + HEAD]
                o_h = o_h * hnw_ref[:, hs] * (gate * _sigmoid(gate))
                o_scr[rows, hs] = o_h.astype(_BF16)
        return carry

    lax.fori_loop(0, MIXER_BLOCK // CHUNK, chunk_body, 0, unroll=True)

    o_ref[0] = x + _dot(o_scr[...], wout_ref[0].astype(_BF16))


def _ffn_kernel(h_ref, nw_ref, wup_ref, cw_ref, cb_ref, wdn_ref, fw_ref, o_ref,
                a_ext, u_scr):
    tb = h_ref.shape[1]

    @pl.when(pl.program_id(1) == 0)
    def _():
        a_ext[0:HALO, :] = jnp.zeros((HALO, D_FF), _F32)

    @pl.when(pl.program_id(1) > 0)
    def _():
        a_ext[0:HALO, :] = a_ext[tb:tb + HALO, :]

    h = h_ref[0]
    hn = h * lax.rsqrt(jnp.mean(h * h, axis=-1, keepdims=True) + EPS) * nw_ref[...]
    hn = hn.astype(_BF16)
    for c0 in range(0, D_FF, FF_TILE):
        cs = slice(c0, c0 + FF_TILE)
        a_ext[HALO:HALO + tb, cs] = _dot(hn, wup_ref[:, cs])
        gate = _dot(hn, wup_ref[:, D_FF + c0:D_FF + c0 + FF_TILE])
        u = (cb_ref[:, cs]
             + a_ext[HALO - 2:HALO - 2 + tb, cs] * cw_ref[0:1, cs]
             + a_ext[HALO - 1:HALO - 1 + tb, cs] * cw_ref[1:2, cs]
             + a_ext[HALO:HALO + tb, cs] * cw_ref[2:3, cs])
        u_scr[:, cs] = (u * _sigmoid(u) * gate).astype(_BF16)
    y = h + _dot(u_scr[...], wdn_ref[...])
    o_ref[0] = y * lax.rsqrt(jnp.mean(y * y, axis=-1, keepdims=True) + EPS) * fw_ref[...]


def _const_spec(shape):
    zeros = (0,) * len(shape)
    return pl.BlockSpec(shape, lambda b, t: zeros, pipeline_mode=pl.Buffered(1))


def kernel(x, attn_norm_w, w_in, hgrn_lower_bounds, w_gla_gate_up, b_gla_gate, hg_norm_w,
           gla_norm_w, w_out, ffn_norm_w, w_up, conv_w, conv_b, w_down, final_norm_w):
    batch, seq, d = x.shape
    assert d == D_MODEL and seq % MIXER_BLOCK == 0 and seq % FFN_BLOCK == 0
    assert w_in.shape[0] == 1, "single-layer problem"

    assert w_in.shape[2] == _GD + GLA_GATE_RANK
    win_t = jnp.transpose(w_in[0])
    wgu = w_gla_gate_up[0]
    wgu_hi = wgu.astype(_BF16)
    wgu_lo = (wgu - wgu_hi.astype(_F32)).astype(_BF16)
    head_nw = jnp.concatenate([hg_norm_w[0], gla_norm_w[0]])[None, :]
    n2_np, mask_np = _level_matrices(CHUNK)
    n2 = jnp.asarray(n2_np, dtype=_BF16)
    masks = jnp.asarray(mask_np)

    params = pltpu.CompilerParams(dimension_semantics=("arbitrary", "arbitrary"),
                                  vmem_limit_bytes=VMEM_LIMIT)

    def act_spec(block):
        return pl.BlockSpec((1, block, D_MODEL), lambda b, t: (b, t, 0))

    steps_per_row = seq // MIXER_BLOCK
    n_steps = batch * steps_per_row
    assert D_MODEL % n_steps == 0 and D_FF % WDN_SLAB == 0 and D_FF // WDN_SLAB <= n_steps
    wup_rows = D_MODEL // n_steps
    last_wdn = D_FF // WDN_SLAB - 1

    def step(b, t):
        return b * steps_per_row + t

    h, w_up_bf16, w_down_bf16 = pl.pallas_call(
        _mixer_kernel,
        grid=(batch, steps_per_row),
        in_specs=[
            act_spec(MIXER_BLOCK),
            _const_spec((1, D_MODEL)),
            _const_spec(win_t.shape),
            _const_spec(hgrn_lower_bounds.shape),
            _const_spec((GLA_GATE_RANK, GLA_KEY_WIDTH)),
            _const_spec((GLA_GATE_RANK, GLA_KEY_WIDTH)),
            _const_spec((1, GLA_KEY_WIDTH)),
            _const_spec((1, MIX)),
            _const_spec(w_out.shape),
            _const_spec(n2.shape),
            _const_spec(masks.shape),
            pl.BlockSpec((1, wup_rows, 2 * D_FF), lambda b, t: (0, step(b, t), 0)),
            pl.BlockSpec((1, WDN_SLAB, D_MODEL),
                         lambda b, t: (0, jnp.minimum(step(b, t), last_wdn), 0)),
        ],
        out_specs=[
            act_spec(MIXER_BLOCK),
            pl.BlockSpec((wup_rows, 2 * D_FF), lambda b, t: (step(b, t), 0)),
            pl.BlockSpec((WDN_SLAB, D_MODEL), lambda b, t: (jnp.minimum(step(b, t), last_wdn), 0)),
        ],
        out_shape=[
            jax.ShapeDtypeStruct(x.shape, _F32),
            jax.ShapeDtypeStruct((D_MODEL, 2 * D_FF), _BF16),
            jax.ShapeDtypeStruct((D_FF, D_MODEL), _BF16),
        ],
        scratch_shapes=[
            pltpu.VMEM((MIXER_BLOCK, PROJ_W), _F32),
            *[pltpu.VMEM((n2.shape[0] + CHUNK, SLAB), _F32)
              for _ in range(N_SLABS)],
            pltpu.VMEM((CHUNK, HG_WIDTH), _F32),
            pltpu.VMEM((N_HEADS, CHUNK, CHUNK), _BF16),
            pltpu.VMEM((N_HEADS, HEAD, LANE), _F32),
            pltpu.VMEM((MIXER_BLOCK, MIX), _BF16),
        ],
        compiler_params=params,
        name="mixer",
    )(x, attn_norm_w, win_t, hgrn_lower_bounds, wgu_hi, wgu_lo, b_gla_gate, head_nw,
      w_out, n2, masks, w_up, w_down)

    out = pl.pallas_call(
        _ffn_kernel,
        grid=(batch, seq // FFN_BLOCK),
        in_specs=[
            act_spec(FFN_BLOCK),
            _const_spec((1, D_MODEL)),
            _const_spec((D_MODEL, 2 * D_FF)),
            _const_spec((3, D_FF)),
            _const_spec((1, D_FF)),
            _const_spec((D_FF, D_MODEL)),
            _const_spec((1, D_MODEL)),
        ],
        out_specs=act_spec(FFN_BLOCK),
        out_shape=jax.ShapeDtypeStruct(x.shape, _F32),
        scratch_shapes=[
            pltpu.VMEM((HALO + FFN_BLOCK, D_FF), _F32),
            pltpu.VMEM((FFN_BLOCK, D_FF), _BF16),
        ],
        compiler_params=params,
        name="ffn",
    )(h, ffn_norm_w, w_up_bf16, conv_w[0], conv_b, w_down_bf16,
      final_norm_w[None, :])
    return out
```

```python
import numpy as np
import jax
import jax.numpy as jnp
from jax import lax
from jax.experimental import pallas as pl
from jax.experimental.pallas import tpu as pltpu

D_MODEL = 1024
HG_WIDTH = 512
GLA_WIDTH = 512
GLA_DK = 64
GLA_KEY_WIDTH = 256
GLA_GATE_RANK = 16
GLA_GATE_TAU = 16.0
D_FF = 2816
EPS = 1e-6
LOG2_E = 1.4426950408889634

LANE = 128
HEAD = 128
N_HEADS = 8
MIX = N_HEADS * HEAD
SLAB = 2 * LANE
N_SLABS = 3
CHUNK = 128
MIXER_BLOCK = 512
FFN_BLOCK = 1024
FF_TILE = 256
WDN_SLAB = 128
SUBLANES = 8
BF16_ROWS = 2 * SUBLANES
HALO = SUBLANES
VMEM_LIMIT = 58 * 1024 * 1024

_HQ, _HF, _HI, _HGT = 0, 512, 1024, 1536
_GQ, _GK, _GV, _GG, _GD = 2048, 2304, 2560, 3072, 3584
PROJ_W = _GD + LANE

_F32 = jnp.float32
_BF16 = jnp.bfloat16


def _level_matrices(c):
    r = np.arange(c)[:, None]
    s = np.arange(c)[None, :]
    blocks = [s <= r]
    masks = [np.eye(c, dtype=bool)]
    m = 1
    while m < c:
        ref = (r // (2 * m)) * (2 * m) + m - 1
        hi = (r // m) % 2 == 1
        if 1 < m < SUBLANES:
            blocks.append(np.where(hi, (s > ref) & (s <= r), (s > r) & (s <= ref)))
        masks.append(((r // (2 * m)) == (s // (2 * m))) & hi & ((s // m) % 2 == 0))
        m *= 2
    n = np.concatenate(blocks, axis=0).astype(np.float32)
    n2 = np.concatenate([n, n], axis=-1)
    return n2, np.stack(masks).astype(np.float32)


def _sigmoid(x):
    return 1.0 / (1.0 + jnp.exp(-x))


def _split_bf16(x):
    hi = x.astype(_BF16)
    lo = (x - hi.astype(_F32)).astype(_BF16)
    return hi, lo


def _dot(a, b):
    return jnp.dot(a, b, preferred_element_type=_F32)


def _dot_nt(a, b):
    return lax.dot_general(a, b, (((1,), (1,)), ((), ())), preferred_element_type=_F32)


def _mixer_kernel(x_ref, nw_ref, win_ref, lb_ref, wgh_ref, wgl_ref, bg_ref, hnw_ref,
                  wout_ref, n2_ref, mask_ref, wup_ref, wdn_ref, o_ref, wup_o_ref, wdn_o_ref,
                  proj_scr, xs0, xs1, xs2, k_all, a_scr, st_scr, o_scr):
    n_levels = mask_ref.shape[0]

    wup_o_ref[...] = wup_ref[0].astype(_BF16)
    wdn_o_ref[...] = wdn_ref[0].astype(_BF16)

    @pl.when(pl.program_id(1) == 0)
    def _():
        st_scr[...] = jnp.zeros_like(st_scr)

    x = x_ref[0]
    xn = x * lax.rsqrt(jnp.mean(x * x, axis=-1, keepdims=True) + EPS) * nw_ref[...]
    xn = xn.astype(_BF16)

    def project(c0, width):
        proj_scr[:, c0:c0 + width] = _dot_nt(xn, win_ref[c0:c0 + width, :].astype(_BF16))

    p = lb_ref[...]
    pe = jnp.exp(p - jnp.max(p, axis=0, keepdims=True))
    lb = pe[0:1, :] / jnp.sum(pe, axis=0, keepdims=True)

    lane = lax.broadcasted_iota(jnp.int32, (1, LANE), 1)
    half_f = [(lane < GLA_DK).astype(_F32), (lane >= GLA_DK).astype(_F32)]
    half = [hf.astype(_BF16) for hf in half_f]
    odd_row = (lax.broadcasted_iota(jnp.int32, (CHUNK, SLAB), 0) & 1) == 1
    n_mxu_rows = n2_ref.shape[0]

    def chunk_body(ci, phase):
        rows = slice(ci * CHUNK, (ci + 1) * CHUNK)
        x_scrs, k_scr = [xs.at[ci] for xs in (xs0, xs1, xs2)], k_all.at[ci]
        slabs = {p: range(N_SLABS) if p == phase else () for p in (1, 2, 3)}

        def load_qk(s):
            if s < 2:
                c0 = s * SLAB
                return proj_scr[rows, _HQ + c0:_HQ + c0 + SLAB], k_scr[:, c0:c0 + SLAB]
            q = proj_scr[rows, _GQ:_GQ + SLAB] * (GLA_DK ** -0.5)
            return q, proj_scr[rows, _GK:_GK + SLAB]

        for s in slabs[1]:
            if s < 2:
                c0 = s * SLAB
                lb_s = lb[:, c0:c0 + SLAB]
                f = lb_s + (1.0 - lb_s) * _sigmoid(proj_scr[rows, _HF + c0:_HF + c0 + SLAB])
                g = jnp.log2(f)
                k_scr[:, c0:c0 + SLAB] = 1.0 - f
            else:
                gd_hi, gd_lo = _split_bf16(proj_scr[rows, _GD:_GD + GLA_GATE_RANK])
                logit = (_dot(gd_hi, wgh_ref[...]) + _dot(gd_lo, wgh_ref[...])
                         + _dot(gd_hi, wgl_ref[...]) + bg_ref[...])
                g = (jnp.minimum(logit, 0.0) * LOG2_E
                     - jnp.log2(1.0 + jnp.exp(-jnp.abs(logit)))) * (1.0 / GLA_GATE_TAU)
            g_hi, g_lo = _split_bf16(g)
            x_scrs[s][0:n_mxu_rows, :] = _dot(n2_ref[...], jnp.concatenate([g_hi, g_lo], axis=0))
            x_scrs[s][n_mxu_rows:n_mxu_rows + CHUNK, :] = g

        def level_exponent(s, lv):
            m = 1 << (lv - 1)
            if m == 1:
                return jnp.where(odd_row, x_scrs[s][n_mxu_rows:n_mxu_rows + CHUNK, :], 0.0)
            if m < SUBLANES:
                return x_scrs[s][(lv - 1) * CHUNK:lv * CHUNK, :]
            cum = x_scrs[s][0:CHUNK, :]
            ref = jnp.concatenate(
                [jnp.broadcast_to(cum[r:r + 1, :], (2 * m, SLAB))
                 for r in range(m - 1, CHUNK, 2 * m)], axis=0)
            return -jnp.abs(cum - ref)

        for s in slabs[2]:
            q_f, k_f = load_qk(s)
            q, k = q_f.astype(_BF16), k_f.astype(_BF16)
            n_pieces = CHUNK // SUBLANES
            acc = [[None] * n_pieces for _ in range(4)]
            for lv in range(n_levels):
                if lv > 0:
                    e = jnp.exp2(level_exponent(s, lv)).astype(_BF16)
                    qp, kp = q * e, k * e
                blk = 1 << max(lv - 1, 0)
                if lv == 0 or blk < SUBLANES:
                    live = list(range(n_pieces))
                else:
                    live = [p for p in range(n_pieces) if (p * SUBLANES // blk) % 2 == 1]
                lhs_rows = live if blk >= BF16_ROWS else list(range(n_pieces))
                for j in range(2):
                    js = slice(j * LANE, (j + 1) * LANE)
                    if lv == 0:
                        prod = q_f[:, js] * k_f[:, js]
                        if s < 2:
                            parts = [jnp.sum(prod, axis=-1, keepdims=True)]
                        else:
                            parts = [jnp.sum(prod * half_f[i], axis=-1, keepdims=True)
                                     for i in range(2)]
                    else:
                        if len(lhs_rows) == n_pieces:
                            q_t = qp[:, js]
                        else:
                            q_t = jnp.concatenate(
                                [qp[r:r + blk, js] for r in range(blk, CHUNK, 2 * blk)], axis=0)
                        n_lhs = q_t.shape[0]
                        if s < 2:
                            parts = [_dot_nt(q_t, kp[:, js])]
                        else:
                            both = jnp.concatenate([q_t * half[0], q_t * half[1]], axis=0)
                            r = _dot_nt(both, kp[:, js])
                            parts = [r[0:n_lhs], r[n_lhs:2 * n_lhs]]
                    for i, part in enumerate(parts):
                        idx = j * len(parts) + i
                        for p in live:
                            src = lhs_rows.index(p) * SUBLANES
                            a = (part[src:src + SUBLANES]
                                 * mask_ref[lv, p * SUBLANES:(p + 1) * SUBLANES, :])
                            acc[idx][p] = a if acc[idx][p] is None else acc[idx][p] + a
            for idx in range(2 if s < 2 else 4):
                a_scr[(2 * s if s < 2 else 4) + idx] = jnp.concatenate(
                    acc[idx], axis=0).astype(_BF16)

        for s in slabs[3]:
            q, k = load_qk(s)
            cum = x_scrs[s][0:CHUNK, :]
            e_q = jnp.exp2(cum)
            e_k = jnp.exp2(cum[CHUNK - 1:CHUNK, :] - cum)
            qd = (q * e_q).astype(_BF16)
            kd = (k * e_k).astype(_BF16)
            chunk_decay = e_q[CHUNK - 1:CHUNK, :]
            for idx in range(2 if s < 2 else 4):
                if s < 2:
                    h = 2 * s + idx
                    js = slice(idx * LANE, (idx + 1) * LANE)
                    qd_h, kd_h = qd[:, js], kd[:, js]
                    v_col, gate_col = _HI + h * HEAD, _HGT + h * HEAD
                else:
                    h = 4 + idx
                    js = slice((idx // 2) * LANE, (idx // 2 + 1) * LANE)
                    qd_h, kd_h = qd[:, js] * half[idx % 2], kd[:, js] * half[idx % 2]
                    v_col, gate_col = _GV + idx * HEAD, _GG + idx * HEAD
                hs = slice(h * HEAD, (h + 1) * HEAD)
                st = st_scr[h]
                v_f = proj_scr[rows, v_col:v_col + HEAD]
                o_h = _dot(a_scr[h], v_f.astype(_BF16)) + _dot_nt(qd_h, st.astype(_BF16))
                st_scr[h] = st * chunk_decay[:, js] + _dot(v_f.T.astype(_BF16), kd_h)
                o_h = o_h * lax.rsqrt(jnp.mean(o_h * o_h, axis=-1, keepdims=True) + EPS)
                gate = proj_scr[rows, gate_col:gate_col + HEAD]
                o_h = o_h * hnw_ref[:, hs] * (gate * _sigmoid(gate))
                o_scr[rows, hs] = o_h.astype(_BF16)

    n_chunks = MIXER_BLOCK // CHUNK
    project(_HF, HG_WIDTH)
    project(_GD, GLA_GATE_RANK)
    rest = [(_HQ, HG_WIDTH), (_HI, HG_WIDTH), (_HGT, HG_WIDTH), (_GQ, 2 * GLA_KEY_WIDTH),
            (_GV, GLA_WIDTH), (_GG, GLA_WIDTH)]
    for i, (c0, width) in enumerate(rest):
        project(c0, width)
        if i < n_chunks:
            chunk_body(i, 1)
    assert n_chunks <= len(rest)
    for ci in range(n_chunks):
        chunk_body(ci, 2)
        chunk_body(ci, 3)

    o_ref[0] = x + _dot(o_scr[...], wout_ref[0].astype(_BF16))


def _ffn_kernel(h_ref, nw_ref, wup_ref, cw_ref, cb_ref, wdn_ref, fw_ref, o_ref,
                a_ext, u_scr):
    tb = h_ref.shape[1]

    @pl.when(pl.program_id(1) == 0)
    def _():
        a_ext[0:HALO, :] = jnp.zeros((HALO, D_FF), _F32)

    @pl.when(pl.program_id(1) > 0)
    def _():
        a_ext[0:HALO, :] = a_ext[tb:tb + HALO, :]

    h = h_ref[0]
    hn = h * lax.rsqrt(jnp.mean(h * h, axis=-1, keepdims=True) + EPS) * nw_ref[...]
    hn = hn.astype(_BF16)
    for c0 in range(0, D_FF, FF_TILE):
        cs = slice(c0, c0 + FF_TILE)
        a_ext[HALO:HALO + tb, cs] = _dot(hn, wup_ref[:, cs])
        gate = _dot(hn, wup_ref[:, D_FF + c0:D_FF + c0 + FF_TILE])
        u = (cb_ref[:, cs]
             + a_ext[HALO - 2:HALO - 2 + tb, cs] * cw_ref[0:1, cs]
             + a_ext[HALO - 1:HALO - 1 + tb, cs] * cw_ref[1:2, cs]
             + a_ext[HALO:HALO + tb, cs] * cw_ref[2:3, cs])
        u_scr[:, cs] = (u * _sigmoid(u) * gate).astype(_BF16)
    y = h + _dot(u_scr[...], wdn_ref[...])
    o_ref[0] = y * lax.rsqrt(jnp.mean(y * y, axis=-1, keepdims=True) + EPS) * fw_ref[...]


def _const_spec(shape):
    zeros = (0,) * len(shape)
    return pl.BlockSpec(shape, lambda b, t: zeros, pipeline_mode=pl.Buffered(1))


def kernel(x, attn_norm_w, w_in, hgrn_lower_bounds, w_gla_gate_up, b_gla_gate, hg_norm_w,
           gla_norm_w, w_out, ffn_norm_w, w_up, conv_w, conv_b, w_down, final_norm_w):
    batch, seq, d = x.shape
    assert d == D_MODEL and seq % MIXER_BLOCK == 0 and seq % FFN_BLOCK == 0
    assert w_in.shape[0] == 1, "single-layer problem"

    assert w_in.shape[2] == _GD + GLA_GATE_RANK
    win_t = jnp.transpose(w_in[0])
    wgu = w_gla_gate_up[0]
    wgu_hi = wgu.astype(_BF16)
    wgu_lo = (wgu - wgu_hi.astype(_F32)).astype(_BF16)
    head_nw = jnp.concatenate([hg_norm_w[0], gla_norm_w[0]])[None, :]
    n2_np, mask_np = _level_matrices(CHUNK)
    n2 = jnp.asarray(n2_np, dtype=_BF16)
    masks = jnp.asarray(mask_np)

    params = pltpu.CompilerParams(dimension_semantics=("arbitrary", "arbitrary"),
                                  vmem_limit_bytes=VMEM_LIMIT)

    def act_spec(block):
        return pl.BlockSpec((1, block, D_MODEL), lambda b, t: (b, t, 0))

    steps_per_row = seq // MIXER_BLOCK
    n_steps = batch * steps_per_row
    assert D_MODEL % n_steps == 0 and D_FF % WDN_SLAB == 0 and D_FF // WDN_SLAB <= n_steps
    wup_rows = D_MODEL // n_steps
    last_wdn = D_FF // WDN_SLAB - 1

    def step(b, t):
        return b * steps_per_row + t

    h, w_up_bf16, w_down_bf16 = pl.pallas_call(
        _mixer_kernel,
        grid=(batch, steps_per_row),
        in_specs=[
            act_spec(MIXER_BLOCK),
            _const_spec((1, D_MODEL)),
            _const_spec(win_t.shape),
            _const_spec(hgrn_lower_bounds.shape),
            _const_spec((GLA_GATE_RANK, GLA_KEY_WIDTH)),
            _const_spec((GLA_GATE_RANK, GLA_KEY_WIDTH)),
            _const_spec((1, GLA_KEY_WIDTH)),
            _const_spec((1, MIX)),
            _const_spec(w_out.shape),
            _const_spec(n2.shape),
            _const_spec(masks.shape),
            pl.BlockSpec((1, wup_rows, 2 * D_FF), lambda b, t: (0, step(b, t), 0)),
            pl.BlockSpec((1, WDN_SLAB, D_MODEL),
                         lambda b, t: (0, jnp.minimum(step(b, t), last_wdn), 0)),
        ],
        out_specs=[
            act_spec(MIXER_BLOCK),
            pl.BlockSpec((wup_rows, 2 * D_FF), lambda b, t: (step(b, t), 0)),
            pl.BlockSpec((WDN_SLAB, D_MODEL), lambda b, t: (jnp.minimum(step(b, t), last_wdn), 0)),
        ],
        out_shape=[
            jax.ShapeDtypeStruct(x.shape, _F32),
            jax.ShapeDtypeStruct((D_MODEL, 2 * D_FF), _BF16),
            jax.ShapeDtypeStruct((D_FF, D_MODEL), _BF16),
        ],
        scratch_shapes=[
            pltpu.VMEM((MIXER_BLOCK, PROJ_W), _F32),
            *[pltpu.VMEM((MIXER_BLOCK // CHUNK, n2.shape[0] + CHUNK, SLAB), _F32)
              for _ in range(N_SLABS)],
            pltpu.VMEM((MIXER_BLOCK // CHUNK, CHUNK, HG_WIDTH), _F32),
            pltpu.VMEM((N_HEADS, CHUNK, CHUNK), _BF16),
            pltpu.VMEM((N_HEADS, HEAD, LANE), _F32),
            pltpu.VMEM((MIXER_BLOCK, MIX), _BF16),
        ],
        compiler_params=params,
        name="mixer",
    )(x, attn_norm_w, win_t, hgrn_lower_bounds, wgu_hi, wgu_lo, b_gla_gate, head_nw,
      w_out, n2, masks, w_up, w_down)

    out = pl.pallas_call(
        _ffn_kernel,
        grid=(batch, seq // FFN_BLOCK),
        in_specs=[
            act_spec(FFN_BLOCK),
            _const_spec((1, D_MODEL)),
            _const_spec((D_MODEL, 2 * D_FF)),
            _const_spec((3, D_FF)),
            _const_spec((1, D_FF)),
            _const_spec((D_FF, D_MODEL)),
            _const_spec((1, D_MODEL)),
        ],
        out_specs=act_spec(FFN_BLOCK),
        out_shape=jax.ShapeDtypeStruct(x.shape, _F32),
        scratch_shapes=[
            pltpu.VMEM((HALO + FFN_BLOCK, D_FF), _F32),
            pltpu.VMEM((FFN_BLOCK, D_FF), _BF16),
        ],
        compiler_params=params,
        name="ffn",
    )(h, ffn_norm_w, w_up_bf16, conv_w[0], conv_b, w_down_bf16,
      final_norm_w[None, :])
    return out
```

```python
import numpy as np
import jax
import jax.numpy as jnp
from jax import lax
from jax.experimental import pallas as pl
from jax.experimental.pallas import tpu as pltpu

D_MODEL = 1024
HG_WIDTH = 512
GLA_WIDTH = 512
GLA_DK = 64
GLA_KEY_WIDTH = 256
GLA_GATE_RANK = 16
GLA_GATE_TAU = 16.0
D_FF = 2816
EPS = 1e-6
LOG2_E = 1.4426950408889634

LANE = 128
HEAD = 128
N_HEADS = 8
MIX = N_HEADS * HEAD
SLAB = 2 * LANE
N_SLABS = 3
KEYS = N_SLABS * SLAB
CHUNK = 128
MIXER_BLOCK = 512
FFN_BLOCK = 1024
FF_TILE = 256
WDN_SLAB = 128
SUBLANES = 8
BF16_ROWS = 2 * SUBLANES
HALO = SUBLANES
VMEM_LIMIT = 58 * 1024 * 1024

_HQ, _HF, _HI, _HGT = 0, 512, 1024, 1536
_GQ, _GK, _GV, _GG, _GD = 2048, 2304, 2560, 3072, 3584
PROJ_W = _GD + LANE

_F32 = jnp.float32
_BF16 = jnp.bfloat16


def _level_matrices(c):
    r = np.arange(c)[:, None]
    s = np.arange(c)[None, :]
    blocks = [s <= r]
    masks = [np.eye(c, dtype=bool)]
    m = 1
    while m < c:
        ref = (r // (2 * m)) * (2 * m) + m - 1
        hi = (r // m) % 2 == 1
        if 1 < m < SUBLANES:
            blocks.append(np.where(hi, (s > ref) & (s <= r), (s > r) & (s <= ref)))
        masks.append(((r // (2 * m)) == (s // (2 * m))) & hi & ((s // m) % 2 == 0))
        m *= 2
    n = np.concatenate(blocks, axis=0).astype(np.float32)
    n2 = np.concatenate([n, n], axis=-1)
    return n2, np.stack(masks).astype(np.float32)


def _sigmoid(x):
    return 1.0 / (1.0 + jnp.exp(-x))


def _split_bf16(x):
    hi = x.astype(_BF16)
    lo = (x - hi.astype(_F32)).astype(_BF16)
    return hi, lo


def _dot(a, b):
    return jnp.dot(a, b, preferred_element_type=_F32)


def _dot_nt(a, b):
    return lax.dot_general(a, b, (((1,), (1,)), ((), ())), preferred_element_type=_F32)


def _mixer_kernel(x_ref, nw_ref, win_ref, lb_ref, wgh_ref, wgl_ref, bg_ref, hnw_ref,
                  wout_ref, n2_ref, mask_ref, wup_ref, wdn_ref, o_ref, wup_o_ref, wdn_o_ref,
                  proj_scr, xs0, xs1, xs2, k_all, qd_scr, kd_scr, cd_scr, g_scr, a_scr, st_scr,
                  o_scr):
    n_levels = mask_ref.shape[0]

    wup_o_ref[...] = wup_ref[0].astype(_BF16)
    wdn_o_ref[...] = wdn_ref[0].astype(_BF16)

    @pl.when(pl.program_id(1) == 0)
    def _():
        st_scr[...] = jnp.zeros_like(st_scr)

    x = x_ref[0]
    xn = x * lax.rsqrt(jnp.mean(x * x, axis=-1, keepdims=True) + EPS) * nw_ref[...]
    xn = xn.astype(_BF16)

    def project(c0, width):
        proj_scr[:, c0:c0 + width] = _dot_nt(xn, win_ref[c0:c0 + width, :].astype(_BF16))

    p = lb_ref[...]
    pe = jnp.exp(p - jnp.max(p, axis=0, keepdims=True))
    lb = pe[0:1, :] / jnp.sum(pe, axis=0, keepdims=True)

    lane = lax.broadcasted_iota(jnp.int32, (1, LANE), 1)
    half_f = [(lane < GLA_DK).astype(_F32), (lane >= GLA_DK).astype(_F32)]
    half = [hf.astype(_BF16) for hf in half_f]
    odd_row = (lax.broadcasted_iota(jnp.int32, (CHUNK, SLAB), 0) & 1) == 1
    n_mxu_rows = n2_ref.shape[0]

    def chunk_body(ci, phase):
        rows = slice(ci * CHUNK, (ci + 1) * CHUNK)
        x_scrs, k_scr = [xs.at[ci] for xs in (xs0, xs1, xs2)], k_all.at[ci]
        slabs = {p: range(N_SLABS) if p == phase else () for p in (1, 2, 3, 4)}

        def load_qk(s):
            if s < 2:
                c0 = s * SLAB
                return proj_scr[rows, _HQ + c0:_HQ + c0 + SLAB], k_scr[:, c0:c0 + SLAB]
            q = proj_scr[rows, _GQ:_GQ + SLAB] * (GLA_DK ** -0.5)
            return q, proj_scr[rows, _GK:_GK + SLAB]

        for s in slabs[1]:
            if s < 2:
                c0 = s * SLAB
                lb_s = lb[:, c0:c0 + SLAB]
                f = lb_s + (1.0 - lb_s) * _sigmoid(proj_scr[rows, _HF + c0:_HF + c0 + SLAB])
                g = jnp.log2(f)
                k_scr[:, c0:c0 + SLAB] = 1.0 - f
            else:
                gd_hi, gd_lo = _split_bf16(proj_scr[rows, _GD:_GD + GLA_GATE_RANK])
                logit = (_dot(gd_hi, wgh_ref[...]) + _dot(gd_lo, wgh_ref[...])
                         + _dot(gd_hi, wgl_ref[...]) + bg_ref[...])
                g = (jnp.minimum(logit, 0.0) * LOG2_E
                     - jnp.log2(1.0 + jnp.exp(-jnp.abs(logit)))) * (1.0 / GLA_GATE_TAU)
            g_hi, g_lo = _split_bf16(g)
            x_scrs[s][0:n_mxu_rows, :] = _dot(n2_ref[...], jnp.concatenate([g_hi, g_lo], axis=0))
            x_scrs[s][n_mxu_rows:n_mxu_rows + CHUNK, :] = g

        def level_exponent(s, lv):
            m = 1 << (lv - 1)
            if m == 1:
                return jnp.where(odd_row, x_scrs[s][n_mxu_rows:n_mxu_rows + CHUNK, :], 0.0)
            if m < SUBLANES:
                return x_scrs[s][(lv - 1) * CHUNK:lv * CHUNK, :]
            cum = x_scrs[s][0:CHUNK, :]
            ref = jnp.concatenate(
                [jnp.broadcast_to(cum[r:r + 1, :], (2 * m, SLAB))
                 for r in range(m - 1, CHUNK, 2 * m)], axis=0)
            return -jnp.abs(cum - ref)

        for s in slabs[2]:
            q_f, k_f = load_qk(s)
            q, k = q_f.astype(_BF16), k_f.astype(_BF16)
            n_pieces = CHUNK // SUBLANES
            acc = [[None] * n_pieces for _ in range(4)]
            for lv in range(n_levels):
                if lv > 0:
                    e = jnp.exp2(level_exponent(s, lv)).astype(_BF16)
                    qp, kp = q * e, k * e
                blk = 1 << max(lv - 1, 0)
                if lv == 0 or blk < SUBLANES:
                    live = list(range(n_pieces))
                else:
                    live = [p for p in range(n_pieces) if (p * SUBLANES // blk) % 2 == 1]
                lhs_rows = live if blk >= BF16_ROWS else list(range(n_pieces))
                for j in range(2):
                    js = slice(j * LANE, (j + 1) * LANE)
                    if lv == 0:
                        prod = q_f[:, js] * k_f[:, js]
                        if s < 2:
                            parts = [jnp.sum(prod, axis=-1, keepdims=True)]
                        else:
                            parts = [jnp.sum(prod * half_f[i], axis=-1, keepdims=True)
                                     for i in range(2)]
                    else:
                        if len(lhs_rows) == n_pieces:
                            q_t = qp[:, js]
                        else:
                            q_t = jnp.concatenate(
                                [qp[r:r + blk, js] for r in range(blk, CHUNK, 2 * blk)], axis=0)
                        n_lhs = q_t.shape[0]
                        if s < 2:
                            parts = [_dot_nt(q_t, kp[:, js])]
                        else:
                            both = jnp.concatenate([q_t * half[0], q_t * half[1]], axis=0)
                            r = _dot_nt(both, kp[:, js])
                            parts = [r[0:n_lhs], r[n_lhs:2 * n_lhs]]
                    for i, part in enumerate(parts):
                        idx = j * len(parts) + i
                        for p in live:
                            src = lhs_rows.index(p) * SUBLANES
                            a = (part[src:src + SUBLANES]
                                 * mask_ref[lv, p * SUBLANES:(p + 1) * SUBLANES, :])
                            acc[idx][p] = a if acc[idx][p] is None else acc[idx][p] + a
            for idx in range(2 if s < 2 else 4):
                a_scr[(2 * s if s < 2 else 4) + idx] = jnp.concatenate(
                    acc[idx], axis=0).astype(_BF16)

        for s in slabs[4]:
            q, k = load_qk(s)
            ks = slice(s * SLAB, (s + 1) * SLAB)
            cum = x_scrs[s][0:CHUNK, :]
            e_q = jnp.exp2(cum)
            e_k = jnp.exp2(cum[CHUNK - 1:CHUNK, :] - cum)
            qd_scr[ci, :, ks] = (q * e_q).astype(_BF16)
            kd_scr[ci, :, ks] = (k * e_k).astype(_BF16)
            cd_scr[ci, 0:1, ks] = e_q[CHUNK - 1:CHUNK, :]

        for s in slabs[3]:
            ks = slice(s * SLAB, (s + 1) * SLAB)
            qd, kd = qd_scr[ci, :, ks], kd_scr[ci, :, ks]
            chunk_decay = cd_scr[ci, 0:1, ks]
            for idx in range(2 if s < 2 else 4):
                if s < 2:
                    h = 2 * s + idx
                    js = slice(idx * LANE, (idx + 1) * LANE)
                    qd_h, kd_h = qd[:, js], kd[:, js]
                    v_col = _HI + h * HEAD
                else:
                    h = 4 + idx
                    js = slice((idx // 2) * LANE, (idx // 2 + 1) * LANE)
                    qd_h, kd_h = qd[:, js] * half[idx % 2], kd[:, js] * half[idx % 2]
                    v_col = _GV + idx * HEAD
                hs = slice(h * HEAD, (h + 1) * HEAD)
                st = st_scr[h]
                v_f = proj_scr[rows, v_col:v_col + HEAD]
                o_h = _dot(a_scr[h], v_f.astype(_BF16)) + _dot(qd_h, st.T.astype(_BF16))
                st_scr[h] = st * chunk_decay[:, js] + _dot(v_f.T.astype(_BF16), kd_h)
                o_h = o_h * lax.rsqrt(jnp.mean(o_h * o_h, axis=-1, keepdims=True) + EPS)
                o_scr[rows, hs] = (o_h * g_scr[rows, hs]).astype(_BF16)

    def output_gates(col, out_col):
        gate = proj_scr[:, col:col + HG_WIDTH]
        g_scr[:, out_col:out_col + HG_WIDTH] = (gate * _sigmoid(gate)
                                                * hnw_ref[:, out_col:out_col + HG_WIDTH])

    n_chunks = MIXER_BLOCK // CHUNK
    assert n_chunks == 4
    project(_HF, HG_WIDTH)
    project(_GD, GLA_GATE_RANK)
    plan = [(_HQ, [(chunk_body, 0, 1)]),
            (_GQ, [(chunk_body, 1, 1)]),
            (_HGT, [(chunk_body, 2, 1)]),
            (_GG, [(chunk_body, 3, 1)]),
            (_HI, [(chunk_body, 0, 4), (chunk_body, 1, 4), (output_gates, _HGT, 0)]),
            (_GV, [(chunk_body, 2, 4), (chunk_body, 3, 4), (output_gates, _GG, HG_WIDTH)])]
    for c0, overlapped in plan:
        project(c0, HG_WIDTH)
        for fn, a, b in overlapped:
            fn(a, b)
    for ci in range(n_chunks):
        chunk_body(ci, 2)
        chunk_body(ci, 3)

    o_ref[0] = x + _dot(o_scr[...], wout_ref[0].astype(_BF16))


def _ffn_kernel(h_ref, nw_ref, wup_ref, cw_ref, cb_ref, wdn_ref, fw_ref, o_ref,
                a_ext, u_scr):
    tb = h_ref.shape[1]

    @pl.when(pl.program_id(1) == 0)
    def _():
        a_ext[0:HALO, :] = jnp.zeros((HALO, D_FF), _F32)

    @pl.when(pl.program_id(1) > 0)
    def _():
        a_ext[0:HALO, :] = a_ext[tb:tb + HALO, :]

    h = h_ref[0]
    hn = h * lax.rsqrt(jnp.mean(h * h, axis=-1, keepdims=True) + EPS) * nw_ref[...]
    hn = hn.astype(_BF16)
    for c0 in range(0, D_FF, FF_TILE):
        cs = slice(c0, c0 + FF_TILE)
        a_ext[HALO:HALO + tb, cs] = _dot(hn, wup_ref[:, cs])
        gate = _dot(hn, wup_ref[:, D_FF + c0:D_FF + c0 + FF_TILE])
        u = (cb_ref[:, cs]
             + a_ext[HALO - 2:HALO - 2 + tb, cs] * cw_ref[0:1, cs]
             + a_ext[HALO - 1:HALO - 1 + tb, cs] * cw_ref[1:2, cs]
             + a_ext[HALO:HALO + tb, cs] * cw_ref[2:3, cs])
        u_scr[:, cs] = (u * _sigmoid(u) * gate).astype(_BF16)
    y = h + _dot(u_scr[...], wdn_ref[...])
    o_ref[0] = y * lax.rsqrt(jnp.mean(y * y, axis=-1, keepdims=True) + EPS) * fw_ref[...]


def _const_spec(shape):
    zeros = (0,) * len(shape)
    return pl.BlockSpec(shape, lambda b, t: zeros, pipeline_mode=pl.Buffered(1))


def kernel(x, attn_norm_w, w_in, hgrn_lower_bounds, w_gla_gate_up, b_gla_gate, hg_norm_w,
           gla_norm_w, w_out, ffn_norm_w, w_up, conv_w, conv_b, w_down, final_norm_w):
    batch, seq, d = x.shape
    assert d == D_MODEL and seq % MIXER_BLOCK == 0 and seq % FFN_BLOCK == 0
    assert w_in.shape[0] == 1, "single-layer problem"

    assert w_in.shape[2] == _GD + GLA_GATE_RANK
    win_t = jnp.transpose(w_in[0])
    wgu = w_gla_gate_up[0]
    wgu_hi = wgu.astype(_BF16)
    wgu_lo = (wgu - wgu_hi.astype(_F32)).astype(_BF16)
    head_nw = jnp.concatenate([hg_norm_w[0], gla_norm_w[0]])[None, :]
    n2_np, mask_np = _level_matrices(CHUNK)
    n2 = jnp.asarray(n2_np, dtype=_BF16)
    masks = jnp.asarray(mask_np)

    params = pltpu.CompilerParams(dimension_semantics=("arbitrary", "arbitrary"),
                                  vmem_limit_bytes=VMEM_LIMIT)

    def act_spec(block):
        return pl.BlockSpec((1, block, D_MODEL), lambda b, t: (b, t, 0))

    steps_per_row = seq // MIXER_BLOCK
    n_steps = batch * steps_per_row
    assert D_MODEL % n_steps == 0 and D_FF % WDN_SLAB == 0 and D_FF // WDN_SLAB <= n_steps
    wup_rows = D_MODEL // n_steps
    last_wdn = D_FF // WDN_SLAB - 1

    def step(b, t):
        return b * steps_per_row + t

    h, w_up_bf16, w_down_bf16 = pl.pallas_call(
        _mixer_kernel,
        grid=(batch, steps_per_row),
        in_specs=[
            act_spec(MIXER_BLOCK),
            _const_spec((1, D_MODEL)),
            _const_spec(win_t.shape),
            _const_spec(hgrn_lower_bounds.shape),
            _const_spec((GLA_GATE_RANK, GLA_KEY_WIDTH)),
            _const_spec((GLA_GATE_RANK, GLA_KEY_WIDTH)),
            _const_spec((1, GLA_KEY_WIDTH)),
            _const_spec((1, MIX)),
            _const_spec(w_out.shape),
            _const_spec(n2.shape),
            _const_spec(masks.shape),
            pl.BlockSpec((1, wup_rows, 2 * D_FF), lambda b, t: (0, step(b, t), 0)),
            pl.BlockSpec((1, WDN_SLAB, D_MODEL),
                         lambda b, t: (0, jnp.minimum(step(b, t), last_wdn), 0)),
        ],
        out_specs=[
            act_spec(MIXER_BLOCK),
            pl.BlockSpec((wup_rows, 2 * D_FF), lambda b, t: (step(b, t), 0)),
            pl.BlockSpec((WDN_SLAB, D_MODEL), lambda b, t: (jnp.minimum(step(b, t), last_wdn), 0)),
        ],
        out_shape=[
            jax.ShapeDtypeStruct(x.shape, _F32),
            jax.ShapeDtypeStruct((D_MODEL, 2 * D_FF), _BF16),
            jax.ShapeDtypeStruct((D_FF, D_MODEL), _BF16),
        ],
        scratch_shapes=[
            pltpu.VMEM((MIXER_BLOCK, PROJ_W), _F32),
            *[pltpu.VMEM((MIXER_BLOCK // CHUNK, n2.shape[0] + CHUNK, SLAB), _F32)
              for _ in range(N_SLABS)],
            pltpu.VMEM((MIXER_BLOCK // CHUNK, CHUNK, HG_WIDTH), _F32),
            pltpu.VMEM((MIXER_BLOCK // CHUNK, CHUNK, KEYS), _BF16),
            pltpu.VMEM((MIXER_BLOCK // CHUNK, CHUNK, KEYS), _BF16),
            pltpu.VMEM((MIXER_BLOCK // CHUNK, SUBLANES, KEYS), _F32),
            pltpu.VMEM((MIXER_BLOCK, MIX), _F32),
            pltpu.VMEM((N_HEADS, CHUNK, CHUNK), _BF16),
            pltpu.VMEM((N_HEADS, HEAD, LANE), _F32),
            pltpu.VMEM((MIXER_BLOCK, MIX), _BF16),
        ],
        compiler_params=params,
        name="mixer",
    )(x, attn_norm_w, win_t, hgrn_lower_bounds, wgu_hi, wgu_lo, b_gla_gate, head_nw,
      w_out, n2, masks, w_up, w_down)

    out = pl.pallas_call(
        _ffn_kernel,
        grid=(batch, seq // FFN_BLOCK),
        in_specs=[
            act_spec(FFN_BLOCK),
            _const_spec((1, D_MODEL)),
            _const_spec((D_MODEL, 2 * D_FF)),
            _const_spec((3, D_FF)),
            _const_spec((1, D_FF)),
            _const_spec((D_FF, D_MODEL)),
            _const_spec((1, D_MODEL)),
        ],
        out_specs=act_spec(FFN_BLOCK),
        out_shape=jax.ShapeDtypeStruct(x.shape, _F32),
        scratch_shapes=[
            pltpu.VMEM((HALO + FFN_BLOCK, D_FF), _F32),
            pltpu.VMEM((FFN_BLOCK, D_FF), _BF16),
        ],
        compiler_params=params,
        name="ffn",
    )(h, ffn_norm_w, w_up_bf16, conv_w[0], conv_b, w_down_bf16,
      final_norm_w[None, :])
    return out
```

```python
import numpy as np
import jax
import jax.numpy as jnp
from jax import lax
from jax.experimental import pallas as pl
from jax.experimental.pallas import tpu as pltpu

D_MODEL = 1024
HG_WIDTH = 512
GLA_WIDTH = 512
GLA_DK = 64
GLA_KEY_WIDTH = 256
GLA_GATE_RANK = 16
GLA_GATE_TAU = 16.0
D_FF = 2816
EPS = 1e-6
LOG2_E = 1.4426950408889634

LANE = 128
HEAD = 128
N_HEADS = 8
MIX = N_HEADS * HEAD
SLAB = 2 * LANE
N_SLABS = 3
KEYS = N_SLABS * SLAB
CHUNK = 128
MIXER_BLOCK = 512
FFN_BLOCK = 1024
FF_TILE = 256
WDN_SLAB = 128
SUBLANES = 8
BF16_ROWS = 2 * SUBLANES
HALO = SUBLANES
VMEM_LIMIT = 58 * 1024 * 1024

_HQ, _HF, _HI, _HGT = 0, 512, 1024, 1536
_GQ, _GK, _GV, _GG, _GD = 2048, 2304, 2560, 3072, 3584
PROJ_W = _GD + LANE

_F32 = jnp.float32
_BF16 = jnp.bfloat16


_SCORE_TERMS = ((0, 0), (1, 1)) + tuple((lv, None) for lv in range(2, CHUNK.bit_length()))


def _level_matrices(c):
    r = np.arange(c)[:, None]
    s = np.arange(c)[None, :]
    blocks = [s <= r]
    level_masks = [np.eye(c, dtype=bool)]
    m = 1
    while m < c:
        ref = (r // (2 * m)) * (2 * m) + m - 1
        hi = (r // m) % 2 == 1
        if 1 < m < SUBLANES:
            blocks.append(np.where(hi, (s > ref) & (s <= r), (s > r) & (s <= ref)))
        level_masks.append(((r // (2 * m)) == (s // (2 * m))) & hi & ((s // m) % 2 == 0))
        m *= 2
    masks = [level_masks[lv] if delta is None else level_masks[lv] & (r - s == delta)
             for lv, delta in _SCORE_TERMS]
    assert (np.sum(masks, axis=0) == (s <= r)).all()
    n = np.concatenate(blocks, axis=0).astype(np.float32)
    n2 = np.concatenate([n, n], axis=-1)
    return n2, np.stack(masks).astype(np.float32)


def _sigmoid(x):
    return 1.0 / (1.0 + jnp.exp(-x))


def _split_bf16(x):
    hi = x.astype(_BF16)
    lo = (x - hi.astype(_F32)).astype(_BF16)
    return hi, lo


def _dot(a, b):
    return jnp.dot(a, b, preferred_element_type=_F32)


def _dot_nt(a, b):
    return lax.dot_general(a, b, (((1,), (1,)), ((), ())), preferred_element_type=_F32)


def _mixer_kernel(x_ref, nw_ref, win_ref, lb_ref, wgh_ref, wgl_ref, bg_ref, hnw_ref,
                  wout_ref, n2_ref, mask_ref, wup_ref, wdn_ref, o_ref, wup_o_ref, wdn_o_ref,
                  proj_scr, xs0, xs1, xs2, k_all, qd_scr, kd_scr, cd_scr, g_scr, a_scr, st_scr,
                  o_scr):

    wup_o_ref[...] = wup_ref[0].astype(_BF16)
    wdn_o_ref[...] = wdn_ref[0].astype(_BF16)

    @pl.when(pl.program_id(1) == 0)
    def _():
        st_scr[...] = jnp.zeros_like(st_scr)

    x = x_ref[0]
    xn = x * lax.rsqrt(jnp.mean(x * x, axis=-1, keepdims=True) + EPS) * nw_ref[...]
    xn = xn.astype(_BF16)

    def project(c0, width):
        proj_scr[:, c0:c0 + width] = _dot_nt(xn, win_ref[c0:c0 + width, :].astype(_BF16))

    p = lb_ref[...]
    pe = jnp.exp(p - jnp.max(p, axis=0, keepdims=True))
    lb = pe[0:1, :] / jnp.sum(pe, axis=0, keepdims=True)

    lane = lax.broadcasted_iota(jnp.int32, (1, LANE), 1)
    half_f = [(lane < GLA_DK).astype(_F32), (lane >= GLA_DK).astype(_F32)]
    half = [hf.astype(_BF16) for hf in half_f]
    n_mxu_rows = n2_ref.shape[0]

    def chunk_body(ci, phase):
        rows = slice(ci * CHUNK, (ci + 1) * CHUNK)
        x_scrs, k_scr = [xs.at[ci] for xs in (xs0, xs1, xs2)], k_all.at[ci]
        slabs = {p: range(N_SLABS) if p == phase else () for p in (1, 2, 3, 4)}

        def load_qk(s):
            if s < 2:
                c0 = s * SLAB
                return proj_scr[rows, _HQ + c0:_HQ + c0 + SLAB], k_scr[:, c0:c0 + SLAB]
            q = proj_scr[rows, _GQ:_GQ + SLAB] * (GLA_DK ** -0.5)
            return q, proj_scr[rows, _GK:_GK + SLAB]

        for s in slabs[1]:
            if s < 2:
                c0 = s * SLAB
                lb_s = lb[:, c0:c0 + SLAB]
                f = lb_s + (1.0 - lb_s) * _sigmoid(proj_scr[rows, _HF + c0:_HF + c0 + SLAB])
                g = jnp.log2(f)
                k_scr[:, c0:c0 + SLAB] = 1.0 - f
            else:
                gd_hi, gd_lo = _split_bf16(proj_scr[rows, _GD:_GD + GLA_GATE_RANK])
                logit = (_dot(gd_hi, wgh_ref[...]) + _dot(gd_lo, wgh_ref[...])
                         + _dot(gd_hi, wgl_ref[...]) + bg_ref[...])
                g = (jnp.minimum(logit, 0.0) * LOG2_E
                     - jnp.log2(1.0 + jnp.exp(-jnp.abs(logit)))) * (1.0 / GLA_GATE_TAU)
            g_hi, g_lo = _split_bf16(g)
            x_scrs[s][0:n_mxu_rows, :] = _dot(n2_ref[...], jnp.concatenate([g_hi, g_lo], axis=0))
            x_scrs[s][n_mxu_rows:n_mxu_rows + CHUNK, :] = g

        def level_exponent(s, lv):
            m = 1 << (lv - 1)
            if m < SUBLANES:
                return x_scrs[s][(lv - 1) * CHUNK:lv * CHUNK, :]
            cum = x_scrs[s][0:CHUNK, :]
            ref = jnp.concatenate(
                [jnp.broadcast_to(cum[r:r + 1, :], (2 * m, SLAB))
                 for r in range(m - 1, CHUNK, 2 * m)], axis=0)
            return -jnp.abs(cum - ref)

        for s in slabs[2]:
            q_f, k_f = load_qk(s)
            q, k = q_f.astype(_BF16), k_f.astype(_BF16)
            n_pieces = CHUNK // SUBLANES
            for j in range(2):
                js = slice(j * LANE, (j + 1) * LANE)
                heads = 1 if s < 2 else 2
                acc = [[None] * n_pieces for _ in range(heads)]
                for mi, (lv, delta) in enumerate(_SCORE_TERMS):
                    blk = 1 << max(lv - 1, 0)
                    if lv == 0 or blk < SUBLANES:
                        live = list(range(n_pieces))
                    else:
                        live = [p for p in range(n_pieces) if (p * SUBLANES // blk) % 2 == 1]
                    lhs_rows = live if blk >= BF16_ROWS else list(range(n_pieces))
                    if delta is not None:
                        if lv == 0:
                            prod = q_f[:, js] * k_f[:, js]
                        else:
                            g_own = x_scrs[s][n_mxu_rows:n_mxu_rows + CHUNK, js]
                            prod = (q_f[:, js] * jnp.exp2(g_own)
                                    * pltpu.roll(k_f[:, js], delta, 0))
                        if s < 2:
                            parts = [jnp.sum(prod, axis=-1, keepdims=True)]
                        else:
                            parts = [jnp.sum(prod * half_f[i], axis=-1, keepdims=True)
                                     for i in range(2)]
                    else:
                        e = jnp.exp2(level_exponent(s, lv)[:, js]).astype(_BF16)
                        qp, kp = q[:, js] * e, k[:, js] * e
                        if len(lhs_rows) == n_pieces:
                            q_t = qp
                        else:
                            q_t = jnp.concatenate(
                                [qp[r:r + blk] for r in range(blk, CHUNK, 2 * blk)], axis=0)
                        n_lhs = q_t.shape[0]
                        if s < 2:
                            parts = [_dot_nt(q_t, kp)]
                        else:
                            both = jnp.concatenate([q_t * half[0], q_t * half[1]], axis=0)
                            r = _dot_nt(both, kp)
                            parts = [r[0:n_lhs], r[n_lhs:2 * n_lhs]]
                    for i, part in enumerate(parts):
                        for p in live:
                            src = lhs_rows.index(p) * SUBLANES
                            a = (part[src:src + SUBLANES]
                                 * mask_ref[mi, p * SUBLANES:(p + 1) * SUBLANES, :])
                            acc[i][p] = a if acc[i][p] is None else acc[i][p] + a
                for i in range(heads):
                    a_scr[(2 * s if s < 2 else 4) + j * heads + i] = jnp.concatenate(
                        acc[i], axis=0).astype(_BF16)

        for s in slabs[4]:
            q, k = load_qk(s)
            ks = slice(s * SLAB, (s + 1) * SLAB)
            cum = x_scrs[s][0:CHUNK, :]
            e_q = jnp.exp2(cum)
            e_k = jnp.exp2(cum[CHUNK - 1:CHUNK, :] - cum)
            qd_scr[ci, :, ks] = (q * e_q).astype(_BF16)
            kd_scr[ci, :, ks] = (k * e_k).astype(_BF16)
            cd_scr[ci, 0:1, ks] = e_q[CHUNK - 1:CHUNK, :]

        for s in slabs[3]:
            ks = slice(s * SLAB, (s + 1) * SLAB)
            qd, kd = qd_scr[ci, :, ks], kd_scr[ci, :, ks]
            chunk_decay = cd_scr[ci, 0:1, ks]
            for idx in range(2 if s < 2 else 4):
                if s < 2:
                    h = 2 * s + idx
                    js = slice(idx * LANE, (idx + 1) * LANE)
                    qd_h, kd_h = qd[:, js], kd[:, js]
                    v_col = _HI + h * HEAD
                else:
                    h = 4 + idx
                    js = slice((idx // 2) * LANE, (idx // 2 + 1) * LANE)
                    qd_h, kd_h = qd[:, js] * half[idx % 2], kd[:, js] * half[idx % 2]
                    v_col = _GV + idx * HEAD
                hs = slice(h * HEAD, (h + 1) * HEAD)
                st = st_scr[h]
                v_f = proj_scr[rows, v_col:v_col + HEAD]
                o_h = _dot(a_scr[h], v_f.astype(_BF16)) + _dot(qd_h, st.T.astype(_BF16))
                st_scr[h] = st * chunk_decay[:, js] + _dot(v_f.T.astype(_BF16), kd_h)
                o_h = o_h * lax.rsqrt(jnp.mean(o_h * o_h, axis=-1, keepdims=True) + EPS)
                o_scr[rows, hs] = (o_h * g_scr[rows, hs]).astype(_BF16)

    def output_gates(col, out_col):
        gate = proj_scr[:, col:col + HG_WIDTH]
        g_scr[:, out_col:out_col + HG_WIDTH] = (gate * _sigmoid(gate)
                                                * hnw_ref[:, out_col:out_col + HG_WIDTH])

    n_chunks = MIXER_BLOCK // CHUNK
    assert n_chunks == 4
    project(_HF, HG_WIDTH)
    project(_GD, GLA_GATE_RANK)
    plan = [(_HQ, [(chunk_body, 0, 1)]),
            (_GQ, [(chunk_body, 1, 1)]),
            (_HGT, [(chunk_body, 2, 1)]),
            (_GG, [(chunk_body, 3, 1)]),
            (_HI, [(chunk_body, 0, 4), (chunk_body, 1, 4), (output_gates, _HGT, 0)]),
            (_GV, [(chunk_body, 2, 4), (chunk_body, 3, 4), (output_gates, _GG, HG_WIDTH)])]
    for c0, overlapped in plan:
        project(c0, HG_WIDTH)
        for fn, a, b in overlapped:
            fn(a, b)
    for ci in range(n_chunks):
        chunk_body(ci, 2)
        chunk_body(ci, 3)

    o_ref[0] = x + _dot(o_scr[...], wout_ref[0].astype(_BF16))


def _ffn_kernel(h_ref, nw_ref, wup_ref, cw_ref, cb_ref, wdn_ref, fw_ref, o_ref,
                a_ext, u_scr):
    tb = h_ref.shape[1]

    @pl.when(pl.program_id(1) == 0)
    def _():
        a_ext[0:HALO, :] = jnp.zeros((HALO, D_FF), _F32)

    @pl.when(pl.program_id(1) > 0)
    def _():
        a_ext[0:HALO, :] = a_ext[tb:tb + HALO, :]

    h = h_ref[0]
    hn = h * lax.rsqrt(jnp.mean(h * h, axis=-1, keepdims=True) + EPS) * nw_ref[...]
    hn = hn.astype(_BF16)
    for c0 in range(0, D_FF, FF_TILE):
        cs = slice(c0, c0 + FF_TILE)
        a_ext[HALO:HALO + tb, cs] = _dot(hn, wup_ref[:, cs])
        gate = _dot(hn, wup_ref[:, D_FF + c0:D_FF + c0 + FF_TILE])
        u = (cb_ref[:, cs]
             + a_ext[HALO - 2:HALO - 2 + tb, cs] * cw_ref[0:1, cs]
             + a_ext[HALO - 1:HALO - 1 + tb, cs] * cw_ref[1:2, cs]
             + a_ext[HALO:HALO + tb, cs] * cw_ref[2:3, cs])
        u_scr[:, cs] = (u * _sigmoid(u) * gate).astype(_BF16)
    y = h + _dot(u_scr[...], wdn_ref[...])
    o_ref[0] = y * lax.rsqrt(jnp.mean(y * y, axis=-1, keepdims=True) + EPS) * fw_ref[...]


def _const_spec(shape):
    zeros = (0,) * len(shape)
    return pl.BlockSpec(shape, lambda b, t: zeros, pipeline_mode=pl.Buffered(1))


def kernel(x, attn_norm_w, w_in, hgrn_lower_bounds, w_gla_gate_up, b_gla_gate, hg_norm_w,
           gla_norm_w, w_out, ffn_norm_w, w_up, conv_w, conv_b, w_down, final_norm_w):
    batch, seq, d = x.shape
    assert d == D_MODEL and seq % MIXER_BLOCK == 0 and seq % FFN_BLOCK == 0
    assert w_in.shape[0] == 1, "single-layer problem"

    assert w_in.shape[2] == _GD + GLA_GATE_RANK
    win_t = jnp.transpose(w_in[0])
    wgu = w_gla_gate_up[0]
    wgu_hi = wgu.astype(_BF16)
    wgu_lo = (wgu - wgu_hi.astype(_F32)).astype(_BF16)
    head_nw = jnp.concatenate([hg_norm_w[0], gla_norm_w[0]])[None, :]
    n2_np, mask_np = _level_matrices(CHUNK)
    n2 = jnp.asarray(n2_np, dtype=_BF16)
    masks = jnp.asarray(mask_np)

    params = pltpu.CompilerParams(dimension_semantics=("arbitrary", "arbitrary"),
                                  vmem_limit_bytes=VMEM_LIMIT)

    def act_spec(block):
        return pl.BlockSpec((1, block, D_MODEL), lambda b, t: (b, t, 0))

    steps_per_row = seq // MIXER_BLOCK
    n_steps = batch * steps_per_row
    assert D_MODEL % n_steps == 0 and D_FF % WDN_SLAB == 0 and D_FF // WDN_SLAB <= n_steps
    wup_rows = D_MODEL // n_steps
    last_wdn = D_FF // WDN_SLAB - 1

    def step(b, t):
        return b * steps_per_row + t

    h, w_up_bf16, w_down_bf16 = pl.pallas_call(
        _mixer_kernel,
        grid=(batch, steps_per_row),
        in_specs=[
            act_spec(MIXER_BLOCK),
            _const_spec((1, D_MODEL)),
            _const_spec(win_t.shape),
            _const_spec(hgrn_lower_bounds.shape),
            _const_spec((GLA_GATE_RANK, GLA_KEY_WIDTH)),
            _const_spec((GLA_GATE_RANK, GLA_KEY_WIDTH)),
            _const_spec((1, GLA_KEY_WIDTH)),
            _const_spec((1, MIX)),
            _const_spec(w_out.shape),
            _const_spec(n2.shape),
            _const_spec(masks.shape),
            pl.BlockSpec((1, wup_rows, 2 * D_FF), lambda b, t: (0, step(b, t), 0)),
            pl.BlockSpec((1, WDN_SLAB, D_MODEL),
                         lambda b, t: (0, jnp.minimum(step(b, t), last_wdn), 0)),
        ],
        out_specs=[
            act_spec(MIXER_BLOCK),
            pl.BlockSpec((wup_rows, 2 * D_FF), lambda b, t: (step(b, t), 0)),
            pl.BlockSpec((WDN_SLAB, D_MODEL), lambda b, t: (jnp.minimum(step(b, t), last_wdn), 0)),
        ],
        out_shape=[
            jax.ShapeDtypeStruct(x.shape, _F32),
            jax.ShapeDtypeStruct((D_MODEL, 2 * D_FF), _BF16),
            jax.ShapeDtypeStruct((D_FF, D_MODEL), _BF16),
        ],
        scratch_shapes=[
            pltpu.VMEM((MIXER_BLOCK, PROJ_W), _F32),
            *[pltpu.VMEM((MIXER_BLOCK // CHUNK, n2.shape[0] + CHUNK, SLAB), _F32)
              for _ in range(N_SLABS)],
            pltpu.VMEM((MIXER_BLOCK // CHUNK, CHUNK, HG_WIDTH), _F32),
            pltpu.VMEM((MIXER_BLOCK // CHUNK, CHUNK, KEYS), _BF16),
            pltpu.VMEM((MIXER_BLOCK // CHUNK, CHUNK, KEYS), _BF16),
            pltpu.VMEM((MIXER_BLOCK // CHUNK, SUBLANES, KEYS), _F32),
            pltpu.VMEM((MIXER_BLOCK, MIX), _F32),
            pltpu.VMEM((N_HEADS, CHUNK, CHUNK), _BF16),
            pltpu.VMEM((N_HEADS, HEAD, LANE), _F32),
            pltpu.VMEM((MIXER_BLOCK, MIX), _BF16),
        ],
        compiler_params=params,
        name="mixer",
    )(x, attn_norm_w, win_t, hgrn_lower_bounds, wgu_hi, wgu_lo, b_gla_gate, head_nw,
      w_out, n2, masks, w_up, w_down)

    out = pl.pallas_call(
        _ffn_kernel,
        grid=(batch, seq // FFN_BLOCK),
        in_specs=[
            act_spec(FFN_BLOCK),
            _const_spec((1, D_MODEL)),
            _const_spec((D_MODEL, 2 * D_FF)),
            _const_spec((3, D_FF)),
            _const_spec((1, D_FF)),
            _const_spec((D_FF, D_MODEL)),
            _const_spec((1, D_MODEL)),
        ],
        out_specs=act_spec(FFN_BLOCK),
        out_shape=jax.ShapeDtypeStruct(x.shape, _F32),
        scratch_shapes=[
            pltpu.VMEM((HALO + FFN_BLOCK, D_FF), _F32),
            pltpu.VMEM((FFN_BLOCK, D_FF), _BF16),
        ],
        compiler_params=params,
        name="ffn",
    )(h, ffn_norm_w, w_up_bf16, conv_w[0], conv_b, w_down_bf16,
      final_norm_w[None, :])
    return out
```

```python
import numpy as np
import jax
import jax.numpy as jnp
from jax import lax
from jax.experimental import pallas as pl
from jax.experimental.pallas import tpu as pltpu

D_MODEL = 1024
HG_WIDTH = 512
GLA_WIDTH = 512
GLA_DK = 64
GLA_KEY_WIDTH = 256
GLA_GATE_RANK = 16
GLA_GATE_TAU = 16.0
D_FF = 2816
EPS = 1e-6
LOG2_E = 1.4426950408889634

LANE = 128
HEAD = 128
N_HEADS = 8
MIX = N_HEADS * HEAD
SLAB = 2 * LANE
N_SLABS = 3
KEYS = N_SLABS * SLAB
CHUNK = 128
MIXER_BLOCK = 512
FFN_BLOCK = 1024
FF_TILE = 256
WDN_SLAB = 128
SUBLANES = 8
BF16_ROWS = 2 * SUBLANES
HALO = SUBLANES
VMEM_LIMIT = 58 * 1024 * 1024

_HQ, _HF, _HI, _HGT = 0, 512, 1024, 1536
_GQ, _GK, _GV, _GG, _GD = 2048, 2304, 2560, 3072, 3584
PROJ_W = _GD + LANE

_F32 = jnp.float32
_BF16 = jnp.bfloat16


_SCORE_TERMS = ((0, 0), (1, 1)) + tuple((lv, None) for lv in range(2, CHUNK.bit_length()))


def _level_matrices(c):
    r = np.arange(c)[:, None]
    s = np.arange(c)[None, :]
    blocks = [s <= r]
    level_masks = [np.eye(c, dtype=bool)]
    m = 1
    while m < c:
        ref = (r // (2 * m)) * (2 * m) + m - 1
        hi = (r // m) % 2 == 1
        if 1 < m < SUBLANES:
            blocks.append(np.where(hi, (s > ref) & (s <= r), (s > r) & (s <= ref)))
        level_masks.append(((r // (2 * m)) == (s // (2 * m))) & hi & ((s // m) % 2 == 0))
        m *= 2
    masks = [level_masks[lv] if delta is None else level_masks[lv] & (r - s == delta)
             for lv, delta in _SCORE_TERMS]
    assert (np.sum(masks, axis=0) == (s <= r)).all()
    n = np.concatenate(blocks, axis=0).astype(np.float32)
    n2 = np.concatenate([n, n], axis=-1)
    return n2, np.stack(masks).astype(np.float32)


def _sigmoid(x):
    return 1.0 / (1.0 + jnp.exp(-x))


def _split_bf16(x):
    hi = x.astype(_BF16)
    lo = (x - hi.astype(_F32)).astype(_BF16)
    return hi, lo


def _dot(a, b):
    return jnp.dot(a, b, preferred_element_type=_F32)


def _dot_nt(a, b):
    return lax.dot_general(a, b, (((1,), (1,)), ((), ())), preferred_element_type=_F32)


def _mixer_kernel(x_ref, nw_ref, win_ref, lb_ref, wgh_ref, wgl_ref, bg_ref, hnw_ref,
                  wout_ref, n2_ref, mask_ref, wup_ref, wdn_ref, o_ref, wup_o_ref, wdn_o_ref,
                  proj_scr, xs0, xs1, xs2, k_all, qd_scr, kd_scr, cd_scr, g_scr, a_scr, st_scr,
                  o_scr):

    @pl.when(pl.program_id(1) == 0)
    def _():
        st_scr[...] = jnp.zeros_like(st_scr)

    x = x_ref[0]
    xn = x * lax.rsqrt(jnp.mean(x * x, axis=-1, keepdims=True) + EPS) * nw_ref[...]
    xn = xn.astype(_BF16)

    def project(c0, width):
        proj_scr[:, c0:c0 + width] = _dot_nt(xn, win_ref[c0:c0 + width, :].astype(_BF16))

    p = lb_ref[...]
    pe = jnp.exp(p - jnp.max(p, axis=0, keepdims=True))
    lb = pe[0:1, :] / jnp.sum(pe, axis=0, keepdims=True)

    lane = lax.broadcasted_iota(jnp.int32, (1, LANE), 1)
    half_f = [(lane < GLA_DK).astype(_F32), (lane >= GLA_DK).astype(_F32)]
    half = [hf.astype(_BF16) for hf in half_f]
    n_mxu_rows = n2_ref.shape[0]

    def chunk_body(ci, phase):
        rows = slice(ci * CHUNK, (ci + 1) * CHUNK)
        x_scrs, k_scr = [xs.at[ci] for xs in (xs0, xs1, xs2)], k_all.at[ci]
        slabs = {p: range(N_SLABS) if p == phase else () for p in (1, 2, 3, 4)}

        def load_qk(s):
            if s < 2:
                c0 = s * SLAB
                return proj_scr[rows, _HQ + c0:_HQ + c0 + SLAB], k_scr[:, c0:c0 + SLAB]
            q = proj_scr[rows, _GQ:_GQ + SLAB] * (GLA_DK ** -0.5)
            return q, proj_scr[rows, _GK:_GK + SLAB]

        for s in slabs[1]:
            if s < 2:
                c0 = s * SLAB
                lb_s = lb[:, c0:c0 + SLAB]
                f = lb_s + (1.0 - lb_s) * _sigmoid(proj_scr[rows, _HF + c0:_HF + c0 + SLAB])
                g = jnp.log2(f)
                k_scr[:, c0:c0 + SLAB] = 1.0 - f
            else:
                gd_hi, gd_lo = _split_bf16(proj_scr[rows, _GD:_GD + GLA_GATE_RANK])
                logit = (_dot(gd_hi, wgh_ref[...]) + _dot(gd_lo, wgh_ref[...])
                         + _dot(gd_hi, wgl_ref[...]) + bg_ref[...])
                g = (jnp.minimum(logit, 0.0) * LOG2_E
                     - jnp.log2(1.0 + jnp.exp(-jnp.abs(logit)))) * (1.0 / GLA_GATE_TAU)
            g_hi, g_lo = _split_bf16(g)
            x_scrs[s][0:n_mxu_rows, :] = _dot(n2_ref[...], jnp.concatenate([g_hi, g_lo], axis=0))
            x_scrs[s][n_mxu_rows:n_mxu_rows + CHUNK, :] = g

        def level_exponent(s, lv):
            m = 1 << (lv - 1)
            if m < SUBLANES:
                return x_scrs[s][(lv - 1) * CHUNK:lv * CHUNK, :]
            cum = x_scrs[s][0:CHUNK, :]
            ref = jnp.concatenate(
                [jnp.broadcast_to(cum[r:r + 1, :], (2 * m, SLAB))
                 for r in range(m - 1, CHUNK, 2 * m)], axis=0)
            return -jnp.abs(cum - ref)

        for s in slabs[2]:
            q_f, k_f = load_qk(s)
            q, k = q_f.astype(_BF16), k_f.astype(_BF16)
            n_pieces = CHUNK // SUBLANES
            for j in range(2):
                js = slice(j * LANE, (j + 1) * LANE)
                heads = 1 if s < 2 else 2
                acc = [[None] * n_pieces for _ in range(heads)]
                for mi, (lv, delta) in enumerate(_SCORE_TERMS):
                    blk = 1 << max(lv - 1, 0)
                    if lv == 0 or blk < SUBLANES:
                        live = list(range(n_pieces))
                    else:
                        live = [p for p in range(n_pieces) if (p * SUBLANES // blk) % 2 == 1]
                    lhs_rows = live if blk >= BF16_ROWS else list(range(n_pieces))
                    if delta is not None:
                        if lv == 0:
                            prod = q_f[:, js] * k_f[:, js]
                        else:
                            g_own = x_scrs[s][n_mxu_rows:n_mxu_rows + CHUNK, js]
                            prod = (q_f[:, js] * jnp.exp2(g_own)
                                    * pltpu.roll(k_f[:, js], delta, 0))
                        if s < 2:
                            parts = [jnp.sum(prod, axis=-1, keepdims=True)]
                        else:
                            parts = [jnp.sum(prod * half_f[i], axis=-1, keepdims=True)
                                     for i in range(2)]
                    else:
                        e = jnp.exp2(level_exponent(s, lv)[:, js]).astype(_BF16)
                        qp, kp = q[:, js] * e, k[:, js] * e
                        if len(lhs_rows) == n_pieces:
                            q_t = qp
                        else:
                            q_t = jnp.concatenate(
                                [qp[r:r + blk] for r in range(blk, CHUNK, 2 * blk)], axis=0)
                        n_lhs = q_t.shape[0]
                        n_keys = CHUNK - blk if 2 * blk == CHUNK else CHUNK
                        kp = kp[0:n_keys]
                        if s < 2:
                            r = _dot_nt(q_t, kp)
                        else:
                            both = jnp.concatenate([q_t * half[0], q_t * half[1]], axis=0)
                            r = _dot_nt(both, kp)
                        if n_keys < CHUNK:
                            r = jnp.concatenate(
                                [r, jnp.zeros((r.shape[0], CHUNK - n_keys), _F32)], axis=1)
                        parts = [r] if s < 2 else [r[0:n_lhs], r[n_lhs:2 * n_lhs]]
                    for i, part in enumerate(parts):
                        for p in live:
                            src = lhs_rows.index(p) * SUBLANES
                            a = part[src:src + SUBLANES]
                            if 2 * blk != CHUNK:
                                a = a * mask_ref[mi, p * SUBLANES:(p + 1) * SUBLANES, :]
                            acc[i][p] = a if acc[i][p] is None else acc[i][p] + a
                for i in range(heads):
                    a_scr[(2 * s if s < 2 else 4) + j * heads + i] = jnp.concatenate(
                        acc[i], axis=0).astype(_BF16)

        for s in slabs[4]:
            q, k = load_qk(s)
            ks = slice(s * SLAB, (s + 1) * SLAB)
            cum = x_scrs[s][0:CHUNK, :]
            e_q = jnp.exp2(cum)
            e_k = jnp.exp2(cum[CHUNK - 1:CHUNK, :] - cum)
            qd_scr[ci, :, ks] = (q * e_q).astype(_BF16)
            kd_scr[ci, :, ks] = (k * e_k).astype(_BF16)
            cd_scr[ci, 0:1, ks] = e_q[CHUNK - 1:CHUNK, :]

        for s in slabs[3]:
            ks = slice(s * SLAB, (s + 1) * SLAB)
            qd, kd = qd_scr[ci, :, ks], kd_scr[ci, :, ks]
            chunk_decay = cd_scr[ci, 0:1, ks]
            for idx in range(2 if s < 2 else 4):
                if s < 2:
                    h = 2 * s + idx
                    js = slice(idx * LANE, (idx + 1) * LANE)
                    qd_h, kd_h = qd[:, js], kd[:, js]
                    v_col = _HI + h * HEAD
                else:
                    h = 4 + idx
                    js = slice((idx // 2) * LANE, (idx // 2 + 1) * LANE)
                    qd_h, kd_h = qd[:, js] * half[idx % 2], kd[:, js] * half[idx % 2]
                    v_col = _GV + idx * HEAD
                hs = slice(h * HEAD, (h + 1) * HEAD)
                st = st_scr[h]
                v_f = proj_scr[rows, v_col:v_col + HEAD]
                o_h = _dot(a_scr[h], v_f.astype(_BF16)) + _dot(qd_h, st.T.astype(_BF16))
                st_scr[h] = st * chunk_decay[:, js] + _dot(v_f.T.astype(_BF16), kd_h)
                o_h = o_h * lax.rsqrt(jnp.mean(o_h * o_h, axis=-1, keepdims=True) + EPS)
                o_scr[rows, hs] = (o_h * g_scr[rows, hs]).astype(_BF16)

    def output_gates(col, out_col):
        gate = proj_scr[:, col:col + HG_WIDTH]
        g_scr[:, out_col:out_col + HG_WIDTH] = (gate * _sigmoid(gate)
                                                * hnw_ref[:, out_col:out_col + HG_WIDTH])

    n_chunks = MIXER_BLOCK // CHUNK
    assert n_chunks == 4
    project(_HF, HG_WIDTH)
    wup_o_ref[...] = wup_ref[0].astype(_BF16)
    wdn_o_ref[...] = wdn_ref[0].astype(_BF16)
    project(_GD, GLA_GATE_RANK)
    plan = [(_HQ, [(chunk_body, 0, 1)]),
            (_GQ, [(chunk_body, 1, 1)]),
            (_HGT, [(chunk_body, 2, 1)]),
            (_GG, [(chunk_body, 3, 1)]),
            (_HI, [(chunk_body, 0, 4), (chunk_body, 1, 4), (output_gates, _HGT, 0)]),
            (_GV, [(chunk_body, 2, 4), (chunk_body, 3, 4), (output_gates, _GG, HG_WIDTH)])]
    for c0, overlapped in plan:
        project(c0, HG_WIDTH)
        for fn, a, b in overlapped:
            fn(a, b)
    for ci in range(n_chunks):
        chunk_body(ci, 2)
        chunk_body(ci, 3)

    o_ref[0] = x + _dot(o_scr[...], wout_ref[0].astype(_BF16))


def _ffn_kernel(h_ref, nw_ref, wup_ref, cw_ref, cb_ref, wdn_ref, fw_ref, o_ref,
                a_ext, u_scr):
    tb = h_ref.shape[1]

    @pl.when(pl.program_id(1) == 0)
    def _():
        a_ext[0:HALO, :] = jnp.zeros((HALO, D_FF), _F32)

    @pl.when(pl.program_id(1) > 0)
    def _():
        a_ext[0:HALO, :] = a_ext[tb:tb + HALO, :]

    h = h_ref[0]
    hn = h * lax.rsqrt(jnp.mean(h * h, axis=-1, keepdims=True) + EPS) * nw_ref[...]
    hn = hn.astype(_BF16)
    for c0 in range(0, D_FF, FF_TILE):
        cs = slice(c0, c0 + FF_TILE)
        a_ext[HALO:HALO + tb, cs] = _dot(hn, wup_ref[:, cs])
        gate = _dot(hn, wup_ref[:, D_FF + c0:D_FF + c0 + FF_TILE])
        u = (cb_ref[:, cs]
             + a_ext[HALO - 2:HALO - 2 + tb, cs] * cw_ref[0:1, cs]
             + a_ext[HALO - 1:HALO - 1 + tb, cs] * cw_ref[1:2, cs]
             + a_ext[HALO:HALO + tb, cs] * cw_ref[2:3, cs])
        u_scr[:, cs] = (u * _sigmoid(u) * gate).astype(_BF16)
    y = h + _dot(u_scr[...], wdn_ref[...])
    o_ref[0] = y * lax.rsqrt(jnp.mean(y * y, axis=-1, keepdims=True) + EPS) * fw_ref[...]


def _const_spec(shape):
    zeros = (0,) * len(shape)
    return pl.BlockSpec(shape, lambda b, t: zeros, pipeline_mode=pl.Buffered(1))


def kernel(x, attn_norm_w, w_in, hgrn_lower_bounds, w_gla_gate_up, b_gla_gate, hg_norm_w,
           gla_norm_w, w_out, ffn_norm_w, w_up, conv_w, conv_b, w_down, final_norm_w):
    batch, seq, d = x.shape
    assert d == D_MODEL and seq % MIXER_BLOCK == 0 and seq % FFN_BLOCK == 0
    assert w_in.shape[0] == 1, "single-layer problem"

    assert w_in.shape[2] == _GD + GLA_GATE_RANK
    win_t = jnp.transpose(w_in[0])
    wgu = w_gla_gate_up[0]
    wgu_hi = wgu.astype(_BF16)
    wgu_lo = (wgu - wgu_hi.astype(_F32)).astype(_BF16)
    head_nw = jnp.concatenate([hg_norm_w[0], gla_norm_w[0]])[None, :]
    n2_np, mask_np = _level_matrices(CHUNK)
    n2 = jnp.asarray(n2_np, dtype=_BF16)
    masks = jnp.asarray(mask_np)

    params = pltpu.CompilerParams(dimension_semantics=("arbitrary", "arbitrary"),
                                  vmem_limit_bytes=VMEM_LIMIT)

    def act_spec(block):
        return pl.BlockSpec((1, block, D_MODEL), lambda b, t: (b, t, 0))

    steps_per_row = seq // MIXER_BLOCK
    n_steps = batch * steps_per_row
    assert D_MODEL % n_steps == 0 and D_FF % WDN_SLAB == 0 and D_FF // WDN_SLAB <= n_steps
    wup_rows = D_MODEL // n_steps
    last_wdn = D_FF // WDN_SLAB - 1

    def step(b, t):
        return b * steps_per_row + t

    h, w_up_bf16, w_down_bf16 = pl.pallas_call(
        _mixer_kernel,
        grid=(batch, steps_per_row),
        in_specs=[
            act_spec(MIXER_BLOCK),
            _const_spec((1, D_MODEL)),
            _const_spec(win_t.shape),
            _const_spec(hgrn_lower_bounds.shape),
            _const_spec((GLA_GATE_RANK, GLA_KEY_WIDTH)),
            _const_spec((GLA_GATE_RANK, GLA_KEY_WIDTH)),
            _const_spec((1, GLA_KEY_WIDTH)),
            _const_spec((1, MIX)),
            _const_spec(w_out.shape),
            _const_spec(n2.shape),
            _const_spec(masks.shape),
            pl.BlockSpec((1, wup_rows, 2 * D_FF), lambda b, t: (0, step(b, t), 0)),
            pl.BlockSpec((1, WDN_SLAB, D_MODEL),
                         lambda b, t: (0, jnp.minimum(step(b, t), last_wdn), 0)),
        ],
        out_specs=[
            act_spec(MIXER_BLOCK),
            pl.BlockSpec((wup_rows, 2 * D_FF), lambda b, t: (step(b, t), 0)),
            pl.BlockSpec((WDN_SLAB, D_MODEL), lambda b, t: (jnp.minimum(step(b, t), last_wdn), 0)),
        ],
        out_shape=[
            jax.ShapeDtypeStruct(x.shape, _F32),
            jax.ShapeDtypeStruct((D_MODEL, 2 * D_FF), _BF16),
            jax.ShapeDtypeStruct((D_FF, D_MODEL), _BF16),
        ],
        scratch_shapes=[
            pltpu.VMEM((MIXER_BLOCK, PROJ_W), _F32),
            *[pltpu.VMEM((MIXER_BLOCK // CHUNK, n2.shape[0] + CHUNK, SLAB), _F32)
              for _ in range(N_SLABS)],
            pltpu.VMEM((MIXER_BLOCK // CHUNK, CHUNK, HG_WIDTH), _F32),
            pltpu.VMEM((MIXER_BLOCK // CHUNK, CHUNK, KEYS), _BF16),
            pltpu.VMEM((MIXER_BLOCK // CHUNK, CHUNK, KEYS), _BF16),
            pltpu.VMEM((MIXER_BLOCK // CHUNK, SUBLANES, KEYS), _F32),
            pltpu.VMEM((MIXER_BLOCK, MIX), _F32),
            pltpu.VMEM((N_HEADS, CHUNK, CHUNK), _BF16),
            pltpu.VMEM((N_HEADS, HEAD, LANE), _F32),
            pltpu.VMEM((MIXER_BLOCK, MIX), _BF16),
        ],
        compiler_params=params,
        name="mixer",
    )(x, attn_norm_w, win_t, hgrn_lower_bounds, wgu_hi, wgu_lo, b_gla_gate, head_nw,
      w_out, n2, masks, w_up, w_down)

    out = pl.pallas_call(
        _ffn_kernel,
        grid=(batch, seq // FFN_BLOCK),
        in_specs=[
            act_spec(FFN_BLOCK),
            _const_spec((1, D_MODEL)),
            _const_spec((D_MODEL, 2 * D_FF)),
            _const_spec((3, D_FF)),
            _const_spec((1, D_FF)),
            _const_spec((D_FF, D_MODEL)),
            _const_spec((1, D_MODEL)),
        ],
        out_specs=act_spec(FFN_BLOCK),
        out_shape=jax.ShapeDtypeStruct(x.shape, _F32),
        scratch_shapes=[
            pltpu.VMEM((HALO + FFN_BLOCK, D_FF), _F32),
            pltpu.VMEM((FFN_BLOCK, D_FF), _BF16),
        ],
        compiler_params=params,
        name="ffn",
    )(h, ffn_norm_w, w_up_bf16, conv_w[0], conv_b, w_down_bf16,
      final_norm_w[None, :])
    return out
```

```python
import numpy as np
import jax
import jax.numpy as jnp
from jax import lax
from jax.experimental import pallas as pl
from jax.experimental.pallas import tpu as pltpu

D_MODEL = 1024
HG_WIDTH = 512
GLA_WIDTH = 512
GLA_DK = 64
GLA_KEY_WIDTH = 256
GLA_GATE_RANK = 16
GLA_GATE_TAU = 16.0
D_FF = 2816
EPS = 1e-6
LOG2_E = 1.4426950408889634

LANE = 128
HEAD = 128
N_HEADS = 8
MIX = N_HEADS * HEAD
SLAB = 2 * LANE
N_SLABS = 3
KEYS = N_SLABS * SLAB
CHUNK = 128
MIXER_BLOCK = 512
FFN_BLOCK = 1024
FF_TILE = 256
WDN_SLAB = 128
SUBLANES = 8
BF16_ROWS = 2 * SUBLANES
HALO = SUBLANES
VMEM_LIMIT = 58 * 1024 * 1024

_HQ, _HF, _HI, _HGT = 0, 512, 1024, 1536
_GQ, _GK, _GV, _GG, _GD = 2048, 2304, 2560, 3072, 3584
PROJ_W = _GD + LANE

_F32 = jnp.float32
_BF16 = jnp.bfloat16


_SCORE_TERMS = ((0, 0), (1, 1)) + tuple((lv, None) for lv in range(2, CHUNK.bit_length()))
_BLOCK_DOT_MIN = CHUNK // 2


def _level_matrices(c):
    r = np.arange(c)[:, None]
    s = np.arange(c)[None, :]
    blocks = [s <= r]
    level_masks = [np.eye(c, dtype=bool)]
    m = 1
    while m < c:
        ref = (r // (2 * m)) * (2 * m) + m - 1
        hi = (r // m) % 2 == 1
        if 1 < m < SUBLANES:
            blocks.append(np.where(hi, (s > ref) & (s <= r), (s > r) & (s <= ref)))
        level_masks.append(((r // (2 * m)) == (s // (2 * m))) & hi & ((s // m) % 2 == 0))
        m *= 2
    masks = [level_masks[lv] if delta is None else level_masks[lv] & (r - s == delta)
             for lv, delta in _SCORE_TERMS]
    assert (np.sum(masks, axis=0) == (s <= r)).all()
    n = np.concatenate(blocks, axis=0).astype(np.float32)
    n2 = np.concatenate([n, n], axis=-1)
    return n2, np.stack(masks).astype(np.float32)


def _sigmoid(x):
    return 1.0 / (1.0 + jnp.exp(-x))


def _split_bf16(x):
    hi = x.astype(_BF16)
    lo = (x - hi.astype(_F32)).astype(_BF16)
    return hi, lo


def _dot(a, b):
    return jnp.dot(a, b, preferred_element_type=_F32)


def _dot_nt(a, b):
    return lax.dot_general(a, b, (((1,), (1,)), ((), ())), preferred_element_type=_F32)


def _mixer_kernel(x_ref, nw_ref, win_ref, lb_ref, wgh_ref, wgl_ref, bg_ref, hnw_ref,
                  wout_ref, n2_ref, mask_ref, wup_ref, wdn_ref, o_ref, wup_o_ref, wdn_o_ref,
                  proj_scr, xs0, xs1, xs2, k_all, qd_scr, kd_scr, cd_scr, g_scr, a_scr, st_scr,
                  o_scr):

    @pl.when(pl.program_id(1) == 0)
    def _():
        st_scr[...] = jnp.zeros_like(st_scr)

    x = x_ref[0]
    xn = x * lax.rsqrt(jnp.mean(x * x, axis=-1, keepdims=True) + EPS) * nw_ref[...]
    xn = xn.astype(_BF16)

    def project(c0, width):
        proj_scr[:, c0:c0 + width] = _dot_nt(xn, win_ref[c0:c0 + width, :].astype(_BF16))

    p = lb_ref[...]
    pe = jnp.exp(p - jnp.max(p, axis=0, keepdims=True))
    lb = pe[0:1, :] / jnp.sum(pe, axis=0, keepdims=True)

    lane = lax.broadcasted_iota(jnp.int32, (1, LANE), 1)
    half_f = [(lane < GLA_DK).astype(_F32), (lane >= GLA_DK).astype(_F32)]
    half = [hf.astype(_BF16) for hf in half_f]
    n_mxu_rows = n2_ref.shape[0]

    def chunk_body(ci, phase):
        rows = slice(ci * CHUNK, (ci + 1) * CHUNK)
        x_scrs, k_scr = [xs.at[ci] for xs in (xs0, xs1, xs2)], k_all.at[ci]
        slabs = {p: range(N_SLABS) if p == phase else () for p in (1, 2, 3, 4)}

        def load_qk(s):
            if s < 2:
                c0 = s * SLAB
                return proj_scr[rows, _HQ + c0:_HQ + c0 + SLAB], k_scr[:, c0:c0 + SLAB]
            q = proj_scr[rows, _GQ:_GQ + SLAB] * (GLA_DK ** -0.5)
            return q, proj_scr[rows, _GK:_GK + SLAB]

        for s in slabs[1]:
            if s < 2:
                c0 = s * SLAB
                lb_s = lb[:, c0:c0 + SLAB]
                f = lb_s + (1.0 - lb_s) * _sigmoid(proj_scr[rows, _HF + c0:_HF + c0 + SLAB])
                g = jnp.log2(f)
                k_scr[:, c0:c0 + SLAB] = 1.0 - f
            else:
                gd_hi, gd_lo = _split_bf16(proj_scr[rows, _GD:_GD + GLA_GATE_RANK])
                logit = (_dot(gd_hi, wgh_ref[...]) + _dot(gd_lo, wgh_ref[...])
                         + _dot(gd_hi, wgl_ref[...]) + bg_ref[...])
                g = (jnp.minimum(logit, 0.0) * LOG2_E
                     - jnp.log2(1.0 + jnp.exp(-jnp.abs(logit)))) * (1.0 / GLA_GATE_TAU)
            g_hi, g_lo = _split_bf16(g)
            x_scrs[s][0:n_mxu_rows, :] = _dot(n2_ref[...], jnp.concatenate([g_hi, g_lo], axis=0))
            x_scrs[s][n_mxu_rows:n_mxu_rows + CHUNK, :] = g

        def level_exponent(s, lv):
            m = 1 << (lv - 1)
            if m < SUBLANES:
                return x_scrs[s][(lv - 1) * CHUNK:lv * CHUNK, :]
            cum = x_scrs[s][0:CHUNK, :]
            ref = jnp.concatenate(
                [jnp.broadcast_to(cum[r:r + 1, :], (2 * m, SLAB))
                 for r in range(m - 1, CHUNK, 2 * m)], axis=0)
            return -jnp.abs(cum - ref)

        for s in slabs[2]:
            q_f, k_f = load_qk(s)
            q, k = q_f.astype(_BF16), k_f.astype(_BF16)
            n_pieces = CHUNK // SUBLANES
            for j in range(2):
                js = slice(j * LANE, (j + 1) * LANE)
                heads = 1 if s < 2 else 2
                acc = [[None] * n_pieces for _ in range(heads)]
                for mi, (lv, delta) in enumerate(_SCORE_TERMS):
                    blk = 1 << max(lv - 1, 0)
                    if lv == 0 or blk < SUBLANES:
                        live = list(range(n_pieces))
                    else:
                        live = [p for p in range(n_pieces) if (p * SUBLANES // blk) % 2 == 1]
                    lhs_rows = list(range(n_pieces))
                    if delta is not None:
                        if lv == 0:
                            prod = q_f[:, js] * k_f[:, js]
                        else:
                            g_own = x_scrs[s][n_mxu_rows:n_mxu_rows + CHUNK, js]
                            prod = (q_f[:, js] * jnp.exp2(g_own)
                                    * pltpu.roll(k_f[:, js], delta, 0))
                        if s < 2:
                            parts = [jnp.sum(prod, axis=-1, keepdims=True)]
                        else:
                            parts = [jnp.sum(prod * half_f[i], axis=-1, keepdims=True)
                                     for i in range(2)]
                    else:
                        e = jnp.exp2(level_exponent(s, lv)[:, js]).astype(_BF16)
                        qp, kp = q[:, js] * e, k[:, js] * e
                        if blk >= _BLOCK_DOT_MIN:
                            for r0 in range(0, CHUNK, 2 * blk):
                                q_b, k_b = qp[r0 + blk:r0 + 2 * blk], kp[r0:r0 + blk]
                                if s >= 2:
                                    q_b = jnp.concatenate([q_b * half[0], q_b * half[1]], axis=0)
                                r = _dot_nt(q_b, k_b)
                                r = jnp.concatenate(
                                    [r, jnp.zeros((r.shape[0], CHUNK - blk), _F32)], axis=1)
                                if r0:
                                    r = pltpu.roll(r, r0, 1)
                                for i in range(heads):
                                    for t in range(blk // SUBLANES):
                                        p = (r0 + blk) // SUBLANES + t
                                        src = i * blk + t * SUBLANES
                                        a = r[src:src + SUBLANES]
                                        acc[i][p] = a if acc[i][p] is None else acc[i][p] + a
                            continue
                        if blk >= BF16_ROWS:
                            lhs_rows = live
                            qp = jnp.concatenate(
                                [qp[r:r + blk] for r in range(blk, CHUNK, 2 * blk)], axis=0)
                        n_lhs = qp.shape[0]
                        if s < 2:
                            parts = [_dot_nt(qp, kp)]
                        else:
                            r = _dot_nt(jnp.concatenate([qp * half[0], qp * half[1]], axis=0), kp)
                            parts = [r[0:n_lhs], r[n_lhs:2 * n_lhs]]
                    for i, part in enumerate(parts):
                        for p in live:
                            src = lhs_rows.index(p) * SUBLANES
                            a = (part[src:src + SUBLANES]
                                 * mask_ref[mi, p * SUBLANES:(p + 1) * SUBLANES, :])
                            acc[i][p] = a if acc[i][p] is None else acc[i][p] + a
                for i in range(heads):
                    a_scr[(2 * s if s < 2 else 4) + j * heads + i] = jnp.concatenate(
                        acc[i], axis=0).astype(_BF16)

        for s in slabs[4]:
            q, k = load_qk(s)
            ks = slice(s * SLAB, (s + 1) * SLAB)
            cum = x_scrs[s][0:CHUNK, :]
            e_q = jnp.exp2(cum)
            e_k = jnp.exp2(cum[CHUNK - 1:CHUNK, :] - cum)
            qd_scr[ci, :, ks] = (q * e_q).astype(_BF16)
            kd_scr[ci, :, ks] = (k * e_k).astype(_BF16)
            cd_scr[ci, 0:1, ks] = e_q[CHUNK - 1:CHUNK, :]

        for s in slabs[3]:
            ks = slice(s * SLAB, (s + 1) * SLAB)
            qd, kd = qd_scr[ci, :, ks], kd_scr[ci, :, ks]
            chunk_decay = cd_scr[ci, 0:1, ks]
            for idx in range(2 if s < 2 else 4):
                if s < 2:
                    h = 2 * s + idx
                    js = slice(idx * LANE, (idx + 1) * LANE)
                    qd_h, kd_h = qd[:, js], kd[:, js]
                    v_col = _HI + h * HEAD
                else:
                    h = 4 + idx
                    js = slice((idx // 2) * LANE, (idx // 2 + 1) * LANE)
                    qd_h, kd_h = qd[:, js] * half[idx % 2], kd[:, js] * half[idx % 2]
                    v_col = _GV + idx * HEAD
                hs = slice(h * HEAD, (h + 1) * HEAD)
                st = st_scr[h]
                v_f = proj_scr[rows, v_col:v_col + HEAD]
                o_h = _dot(a_scr[h], v_f.astype(_BF16)) + _dot(qd_h, st.T.astype(_BF16))
                st_scr[h] = st * chunk_decay[:, js] + _dot(v_f.T.astype(_BF16), kd_h)
                o_h = o_h * lax.rsqrt(jnp.mean(o_h * o_h, axis=-1, keepdims=True) + EPS)
                o_scr[rows, hs] = (o_h * g_scr[rows, hs]).astype(_BF16)

    def output_gates(col, out_col):
        gate = proj_scr[:, col:col + HG_WIDTH]
        g_scr[:, out_col:out_col + HG_WIDTH] = (gate * _sigmoid(gate)
                                                * hnw_ref[:, out_col:out_col + HG_WIDTH])

    n_chunks = MIXER_BLOCK // CHUNK
    assert n_chunks == 4
    project(_HF, HG_WIDTH)
    wup_o_ref[...] = wup_ref[0].astype(_BF16)
    wdn_o_ref[...] = wdn_ref[0].astype(_BF16)
    project(_GD, GLA_GATE_RANK)
    plan = [(_HQ, [(chunk_body, 0, 1)]),
            (_GQ, [(chunk_body, 1, 1)]),
            (_HGT, [(chunk_body, 2, 1)]),
            (_GG, [(chunk_body, 3, 1)]),
            (_HI, [(chunk_body, 0, 4), (chunk_body, 1, 4), (output_gates, _HGT, 0)]),
            (_GV, [(chunk_body, 2, 4), (chunk_body, 3, 4), (output_gates, _GG, HG_WIDTH)])]
    for c0, overlapped in plan:
        project(c0, HG_WIDTH)
        for fn, a, b in overlapped:
            fn(a, b)
    for ci in range(n_chunks):
        chunk_body(ci, 2)
        chunk_body(ci, 3)

    o_ref[0] = x + _dot(o_scr[...], wout_ref[0].astype(_BF16))


def _ffn_kernel(h_ref, nw_ref, wup_ref, cw_ref, cb_ref, wdn_ref, fw_ref, o_ref,
                a_ext, u_scr):
    tb = h_ref.shape[1]

    @pl.when(pl.program_id(1) == 0)
    def _():
        a_ext[0:HALO, :] = jnp.zeros((HALO, D_FF), _F32)

    @pl.when(pl.program_id(1) > 0)
    def _():
        a_ext[0:HALO, :] = a_ext[tb:tb + HALO, :]

    h = h_ref[0]
    hn = h * lax.rsqrt(jnp.mean(h * h, axis=-1, keepdims=True) + EPS) * nw_ref[...]
    hn = hn.astype(_BF16)
    for c0 in range(0, D_FF, FF_TILE):
        cs = slice(c0, c0 + FF_TILE)
        a_ext[HALO:HALO + tb, cs] = _dot(hn, wup_ref[:, cs])
        gate = _dot(hn, wup_ref[:, D_FF + c0:D_FF + c0 + FF_TILE])
        u = (cb_ref[:, cs]
             + a_ext[HALO - 2:HALO - 2 + tb, cs] * cw_ref[0:1, cs]
             + a_ext[HALO - 1:HALO - 1 + tb, cs] * cw_ref[1:2, cs]
             + a_ext[HALO:HALO + tb, cs] * cw_ref[2:3, cs])
        u_scr[:, cs] = (u * _sigmoid(u) * gate).astype(_BF16)
    for r0 in range(0, tb, tb // 4):
        rows = slice(r0, r0 + tb // 4)
        ssq = None
        for c0 in range(0, D_MODEL, FF_TILE):
            cs = slice(c0, c0 + FF_TILE)
            y = h_ref[0, rows, cs] + _dot(u_scr[rows, :], wdn_ref[:, cs])
            o_ref[0, rows, cs] = y
            part = jnp.sum(y * y, axis=-1, keepdims=True)
            ssq = part if ssq is None else ssq + part
        r = lax.rsqrt(ssq * (1.0 / D_MODEL) + EPS)
        o_ref[0, rows, :] = o_ref[0, rows, :] * r * fw_ref[...]


def _const_spec(shape):
    zeros = (0,) * len(shape)
    return pl.BlockSpec(shape, lambda b, t: zeros, pipeline_mode=pl.Buffered(1))


def kernel(x, attn_norm_w, w_in, hgrn_lower_bounds, w_gla_gate_up, b_gla_gate, hg_norm_w,
           gla_norm_w, w_out, ffn_norm_w, w_up, conv_w, conv_b, w_down, final_norm_w):
    batch, seq, d = x.shape
    assert d == D_MODEL and seq % MIXER_BLOCK == 0 and seq % FFN_BLOCK == 0
    assert w_in.shape[0] == 1, "single-layer problem"

    assert w_in.shape[2] == _GD + GLA_GATE_RANK
    win_t = jnp.transpose(w_in[0])
    wgu = w_gla_gate_up[0]
    wgu_hi = wgu.astype(_BF16)
    wgu_lo = (wgu - wgu_hi.astype(_F32)).astype(_BF16)
    head_nw = jnp.concatenate([hg_norm_w[0], gla_norm_w[0]])[None, :]
    n2_np, mask_np = _level_matrices(CHUNK)
    n2 = jnp.asarray(n2_np, dtype=_BF16)
    masks = jnp.asarray(mask_np)

    params = pltpu.CompilerParams(dimension_semantics=("arbitrary", "arbitrary"),
                                  vmem_limit_bytes=VMEM_LIMIT)

    def act_spec(block):
        return pl.BlockSpec((1, block, D_MODEL), lambda b, t: (b, t, 0))

    steps_per_row = seq // MIXER_BLOCK
    n_steps = batch * steps_per_row
    assert D_MODEL % n_steps == 0 and D_FF % WDN_SLAB == 0 and D_FF // WDN_SLAB <= n_steps
    wup_rows = D_MODEL // n_steps
    last_wdn = D_FF // WDN_SLAB - 1

    def step(b, t):
        return b * steps_per_row + t

    h, w_up_bf16, w_down_bf16 = pl.pallas_call(
        _mixer_kernel,
        grid=(batch, steps_per_row),
        in_specs=[
            act_spec(MIXER_BLOCK),
            _const_spec((1, D_MODEL)),
            _const_spec(win_t.shape),
            _const_spec(hgrn_lower_bounds.shape),
            _const_spec((GLA_GATE_RANK, GLA_KEY_WIDTH)),
            _const_spec((GLA_GATE_RANK, GLA_KEY_WIDTH)),
            _const_spec((1, GLA_KEY_WIDTH)),
            _const_spec((1, MIX)),
            _const_spec(w_out.shape),
            _const_spec(n2.shape),
            _const_spec(masks.shape),
            pl.BlockSpec((1, wup_rows, 2 * D_FF), lambda b, t: (0, step(b, t), 0)),
            pl.BlockSpec((1, WDN_SLAB, D_MODEL),
                         lambda b, t: (0, jnp.minimum(step(b, t), last_wdn), 0)),
        ],
        out_specs=[
            act_spec(MIXER_BLOCK),
            pl.BlockSpec((wup_rows, 2 * D_FF), lambda b, t: (step(b, t), 0)),
            pl.BlockSpec((WDN_SLAB, D_MODEL), lambda b, t: (jnp.minimum(step(b, t), last_wdn), 0)),
        ],
        out_shape=[
            jax.ShapeDtypeStruct(x.shape, _F32),
            jax.ShapeDtypeStruct((D_MODEL, 2 * D_FF), _BF16),
            jax.ShapeDtypeStruct((D_FF, D_MODEL), _BF16),
        ],
        scratch_shapes=[
            pltpu.VMEM((MIXER_BLOCK, PROJ_W), _F32),
            *[pltpu.VMEM((MIXER_BLOCK // CHUNK, n2.shape[0] + CHUNK, SLAB), _F32)
              for _ in range(N_SLABS)],
            pltpu.VMEM((MIXER_BLOCK // CHUNK, CHUNK, HG_WIDTH), _F32),
            pltpu.VMEM((MIXER_BLOCK // CHUNK, CHUNK, KEYS), _BF16),
            pltpu.VMEM((MIXER_BLOCK // CHUNK, CHUNK, KEYS), _BF16),
            pltpu.VMEM((MIXER_BLOCK // CHUNK, SUBLANES, KEYS), _F32),
            pltpu.VMEM((MIXER_BLOCK, MIX), _F32),
            pltpu.VMEM((N_HEADS, CHUNK, CHUNK), _BF16),
            pltpu.VMEM((N_HEADS, HEAD, LANE), _F32),
            pltpu.VMEM((MIXER_BLOCK, MIX), _BF16),
        ],
        compiler_params=params,
        name="mixer",
    )(x, attn_norm_w, win_t, hgrn_lower_bounds, wgu_hi, wgu_lo, b_gla_gate, head_nw,
      w_out, n2, masks, w_up, w_down)

    out = pl.pallas_call(
        _ffn_kernel,
        grid=(batch, seq // FFN_BLOCK),
        in_specs=[
            act_spec(FFN_BLOCK),
            _const_spec((1, D_MODEL)),
            _const_spec((D_MODEL, 2 * D_FF)),
            _const_spec((3, D_FF)),
            _const_spec((1, D_FF)),
            _const_spec((D_FF, D_MODEL)),
            _const_spec((1, D_MODEL)),
        ],
        out_specs=act_spec(FFN_BLOCK),
        out_shape=jax.ShapeDtypeStruct(x.shape, _F32),
        scratch_shapes=[
            pltpu.VMEM((HALO + FFN_BLOCK, D_FF), _F32),
            pltpu.VMEM((FFN_BLOCK, D_FF), _BF16),
        ],
        compiler_params=params,
        name="ffn",
    )(h, ffn_norm_w, w_up_bf16, conv_w[0], conv_b, w_down_bf16,
      final_norm_w[None, :])
    return out
```

```python
import numpy as np
import jax
import jax.numpy as jnp
from jax import lax
from jax.experimental import pallas as pl
from jax.experimental.pallas import tpu as pltpu

D_MODEL = 1024
HG_WIDTH = 512
GLA_WIDTH = 512
GLA_DK = 64
GLA_KEY_WIDTH = 256
GLA_GATE_RANK = 16
GLA_GATE_TAU = 16.0
D_FF = 2816
EPS = 1e-6
LOG2_E = 1.4426950408889634

LANE = 128
HEAD = 128
N_HEADS = 8
MIX = N_HEADS * HEAD
SLAB = 2 * LANE
N_SLABS = 3
KEYS = N_SLABS * SLAB
CHUNK = 128
MIXER_BLOCK = 512
FFN_BLOCK = 1024
FF_TILE = 256
WDN_SLAB = 128
SUBLANES = 8
BF16_ROWS = 2 * SUBLANES
HALO = SUBLANES
VMEM_LIMIT = 58 * 1024 * 1024

_HQ, _HF, _HI, _HGT = 0, 512, 1024, 1536
_GQ, _GK, _GV, _GG, _GD = 2048, 2304, 2560, 3072, 3584
PROJ_W = _GD + LANE

_F32 = jnp.float32
_BF16 = jnp.bfloat16


_SCORE_TERMS = ((0, 0), (1, 1)) + tuple((lv, None) for lv in range(2, CHUNK.bit_length()))
_BLOCK_DOT_MIN = CHUNK // 2


def _level_matrices(c):
    r = np.arange(c)[:, None]
    s = np.arange(c)[None, :]
    blocks = [s <= r]
    level_masks = [np.eye(c, dtype=bool)]
    m = 1
    while m < c:
        ref = (r // (2 * m)) * (2 * m) + m - 1
        hi = (r // m) % 2 == 1
        if 1 < m < SUBLANES:
            blocks.append(np.where(hi, (s > ref) & (s <= r), (s > r) & (s <= ref)))
        level_masks.append(((r // (2 * m)) == (s // (2 * m))) & hi & ((s // m) % 2 == 0))
        m *= 2
    masks = [level_masks[lv] if delta is None else level_masks[lv] & (r - s == delta)
             for lv, delta in _SCORE_TERMS]
    assert (np.sum(masks, axis=0) == (s <= r)).all()
    n = np.concatenate(blocks, axis=0).astype(np.float32)
    n2 = np.concatenate([n, n], axis=-1)
    return n2, np.stack(masks).astype(np.float32)


def _sigmoid(x):
    return 1.0 / (1.0 + jnp.exp(-x))


def _split_bf16(x):
    hi = x.astype(_BF16)
    lo = (x - hi.astype(_F32)).astype(_BF16)
    return hi, lo


def _dot(a, b):
    return jnp.dot(a, b, preferred_element_type=_F32)


def _dot_nt(a, b):
    return lax.dot_general(a, b, (((1,), (1,)), ((), ())), preferred_element_type=_F32)


def _mixer_kernel(x_ref, nw_ref, win_ref, lb_ref, wgh_ref, wgl_ref, bg_ref, hnw_ref,
                  wout_ref, n2_ref, mask_ref, wup_ref, wdn_ref, o_ref, wup_o_ref, wdn_o_ref,
                  proj_scr, xs0, xs1, xs2, k_all, qd_scr, kd_scr, cd_scr, g_scr, a_scr, st_scr,
                  o_scr):

    @pl.when(pl.program_id(1) == 0)
    def _():
        st_scr[...] = jnp.zeros_like(st_scr)

    x = x_ref[0]
    xn = x * lax.rsqrt(jnp.mean(x * x, axis=-1, keepdims=True) + EPS) * nw_ref[...]
    xn = xn.astype(_BF16)

    def project(c0, width):
        proj_scr[:, c0:c0 + width] = _dot_nt(xn, win_ref[c0:c0 + width, :].astype(_BF16))

    p = lb_ref[...]
    pe = jnp.exp(p - jnp.max(p, axis=0, keepdims=True))
    lb = pe[0:1, :] / jnp.sum(pe, axis=0, keepdims=True)

    lane = lax.broadcasted_iota(jnp.int32, (1, LANE), 1)
    half_f = [(lane < GLA_DK).astype(_F32), (lane >= GLA_DK).astype(_F32)]
    half = [hf.astype(_BF16) for hf in half_f]
    n_mxu_rows = n2_ref.shape[0]

    def chunk_body(ci, phase):
        rows = slice(ci * CHUNK, (ci + 1) * CHUNK)
        x_scrs, k_scr = [xs.at[ci] for xs in (xs0, xs1, xs2)], k_all.at[ci]
        slabs = {p: range(N_SLABS) if p == phase else () for p in (1, 2, 3, 4)}

        def load_qk(s):
            if s < 2:
                c0 = s * SLAB
                return proj_scr[rows, _HQ + c0:_HQ + c0 + SLAB], k_scr[:, c0:c0 + SLAB]
            q = proj_scr[rows, _GQ:_GQ + SLAB] * (GLA_DK ** -0.5)
            return q, proj_scr[rows, _GK:_GK + SLAB]

        for s in slabs[1]:
            if s < 2:
                c0 = s * SLAB
                lb_s = lb[:, c0:c0 + SLAB]
                f = lb_s + (1.0 - lb_s) * _sigmoid(proj_scr[rows, _HF + c0:_HF + c0 + SLAB])
                g = jnp.log2(f)
                k_scr[:, c0:c0 + SLAB] = 1.0 - f
            else:
                gd_hi, gd_lo = _split_bf16(proj_scr[rows, _GD:_GD + GLA_GATE_RANK])
                logit = (_dot(gd_hi, wgh_ref[...]) + _dot(gd_lo, wgh_ref[...])
                         + _dot(gd_hi, wgl_ref[...]) + bg_ref[...])
                g = (jnp.minimum(logit, 0.0) * LOG2_E
                     - jnp.log2(1.0 + jnp.exp(-jnp.abs(logit)))) * (1.0 / GLA_GATE_TAU)
            g_hi, g_lo = _split_bf16(g)
            x_scrs[s][0:n_mxu_rows, :] = _dot(n2_ref[...], jnp.concatenate([g_hi, g_lo], axis=0))
            x_scrs[s][n_mxu_rows:n_mxu_rows + CHUNK, :] = g

        def level_exponent(s, lv):
            m = 1 << (lv - 1)
            if m < SUBLANES:
                return x_scrs[s][(lv - 1) * CHUNK:lv * CHUNK, :]
            cum = x_scrs[s][0:CHUNK, :]
            ref = jnp.concatenate(
                [jnp.broadcast_to(cum[r:r + 1, :], (2 * m, SLAB))
                 for r in range(m - 1, CHUNK, 2 * m)], axis=0)
            return -jnp.abs(cum - ref)

        for s in slabs[2]:
            q_f, k_f = load_qk(s)
            q, k = q_f.astype(_BF16), k_f.astype(_BF16)
            n_pieces = CHUNK // SUBLANES
            for j in range(2):
                js = slice(j * LANE, (j + 1) * LANE)
                heads = 1 if s < 2 else 2
                acc = [[None] * n_pieces for _ in range(heads)]
                for mi, (lv, delta) in enumerate(_SCORE_TERMS):
                    blk = 1 << max(lv - 1, 0)
                    if lv == 0 or blk < SUBLANES:
                        live = list(range(n_pieces))
                    else:
                        live = [p for p in range(n_pieces) if (p * SUBLANES // blk) % 2 == 1]
                    lhs_rows = list(range(n_pieces))
                    if delta is not None:
                        if lv == 0:
                            prod = q_f[:, js] * k_f[:, js]
                        else:
                            g_own = x_scrs[s][n_mxu_rows:n_mxu_rows + CHUNK, js]
                            prod = (q_f[:, js] * jnp.exp2(g_own)
                                    * pltpu.roll(k_f[:, js], delta, 0))
                        if s < 2:
                            parts = [jnp.sum(prod, axis=-1, keepdims=True)]
                        else:
                            parts = [jnp.sum(prod * half_f[i], axis=-1, keepdims=True)
                                     for i in range(2)]
                    else:
                        e = jnp.exp2(level_exponent(s, lv)[:, js]).astype(_BF16)
                        qp, kp = q[:, js] * e, k[:, js] * e
                        if blk >= _BLOCK_DOT_MIN:
                            for r0 in range(0, CHUNK, 2 * blk):
                                q_b, k_b = qp[r0 + blk:r0 + 2 * blk], kp[r0:r0 + blk]
                                if s >= 2:
                                    q_b = jnp.concatenate([q_b * half[0], q_b * half[1]], axis=0)
                                r = _dot_nt(q_b, k_b)
                                r = jnp.concatenate(
                                    [r, jnp.zeros((r.shape[0], CHUNK - blk), _F32)], axis=1)
                                if r0:
                                    r = pltpu.roll(r, r0, 1)
                                for i in range(heads):
                                    for t in range(blk // SUBLANES):
                                        p = (r0 + blk) // SUBLANES + t
                                        src = i * blk + t * SUBLANES
                                        a = r[src:src + SUBLANES]
                                        acc[i][p] = a if acc[i][p] is None else acc[i][p] + a
                            continue
                        if blk >= BF16_ROWS:
                            lhs_rows = live
                            qp = jnp.concatenate(
                                [qp[r:r + blk] for r in range(blk, CHUNK, 2 * blk)], axis=0)
                        n_lhs = qp.shape[0]
                        if s < 2:
                            parts = [_dot_nt(qp, kp)]
                        else:
                            r = _dot_nt(jnp.concatenate([qp * half[0], qp * half[1]], axis=0), kp)
                            parts = [r[0:n_lhs], r[n_lhs:2 * n_lhs]]
                    for i, part in enumerate(parts):
                        for p in live:
                            src = lhs_rows.index(p) * SUBLANES
                            a = (part[src:src + SUBLANES]
                                 * mask_ref[mi, p * SUBLANES:(p + 1) * SUBLANES, :])
                            acc[i][p] = a if acc[i][p] is None else acc[i][p] + a
                for i in range(heads):
                    a_scr[(2 * s if s < 2 else 4) + j * heads + i] = jnp.concatenate(
                        acc[i], axis=0).astype(_BF16)

        for s in slabs[4]:
            q, k = load_qk(s)
            ks = slice(s * SLAB, (s + 1) * SLAB)
            cum = x_scrs[s][0:CHUNK, :]
            e_q = jnp.exp2(cum)
            e_k = jnp.exp2(cum[CHUNK - 1:CHUNK, :] - cum)
            qd_scr[ci, :, ks] = (q * e_q).astype(_BF16)
            kd_scr[ci, :, ks] = (k * e_k).astype(_BF16)
            cd_scr[ci, 0:1, ks] = e_q[CHUNK - 1:CHUNK, :]

        for s in slabs[3]:
            ks = slice(s * SLAB, (s + 1) * SLAB)
            qd, kd = qd_scr[ci, :, ks], kd_scr[ci, :, ks]
            chunk_decay = cd_scr[ci, 0:1, ks]
            for idx in range(2 if s < 2 else 4):
                if s < 2:
                    h = 2 * s + idx
                    js = slice(idx * LANE, (idx + 1) * LANE)
                    qd_h, kd_h = qd[:, js], kd[:, js]
                    v_col = _HI + h * HEAD
                else:
                    h = 4 + idx
                    js = slice((idx // 2) * LANE, (idx // 2 + 1) * LANE)
                    qd_h, kd_h = qd[:, js] * half[idx % 2], kd[:, js] * half[idx % 2]
                    v_col = _GV + idx * HEAD
                hs = slice(h * HEAD, (h + 1) * HEAD)
                st = st_scr[h]
                v_f = proj_scr[rows, v_col:v_col + HEAD]
                o_h = _dot(a_scr[h], v_f.astype(_BF16)) + _dot(qd_h, st.T.astype(_BF16))
                st_scr[h] = st * chunk_decay[:, js] + _dot(v_f.T.astype(_BF16), kd_h)
                o_h = o_h * lax.rsqrt(jnp.mean(o_h * o_h, axis=-1, keepdims=True) + EPS)
                o_scr[rows, hs] = (o_h * g_scr[rows, hs]).astype(_BF16)

    def output_gates(col, out_col):
        gate = proj_scr[:, col:col + HG_WIDTH]
        g_scr[:, out_col:out_col + HG_WIDTH] = (gate * _sigmoid(gate)
                                                * hnw_ref[:, out_col:out_col + HG_WIDTH])

    n_chunks = MIXER_BLOCK // CHUNK
    assert n_chunks == 4
    project(_HF, HG_WIDTH)
    wup_o_ref[...] = wup_ref[0].astype(_BF16)
    wdn_o_ref[...] = wdn_ref[0].astype(_BF16)
    project(_GD, GLA_GATE_RANK)
    plan = [(_HQ, [(chunk_body, 0, 1)]),
            (_GQ, [(chunk_body, 1, 1)]),
            (_HGT, [(chunk_body, 2, 1)]),
            (_GG, [(chunk_body, 3, 1)]),
            (_HI, [(chunk_body, 0, 4), (chunk_body, 1, 4), (output_gates, _HGT, 0)]),
            (_GV, [(chunk_body, 2, 4), (chunk_body, 3, 4), (output_gates, _GG, HG_WIDTH)])]
    for c0, overlapped in plan:
        project(c0, HG_WIDTH)
        for fn, a, b in overlapped:
            fn(a, b)
    for ci in range(n_chunks):
        chunk_body(ci, 2)
        chunk_body(ci, 3)

    o_ref[0] = x + _dot(o_scr[...], wout_ref[0].astype(_BF16))


def _ffn_kernel(h_ref, nw_ref, wup_ref, cw_ref, cb_ref, wdn_ref, fw_ref, o_ref,
                a_ext, u_scr):
    tb = h_ref.shape[1]

    @pl.when(pl.program_id(1) == 0)
    def _():
        a_ext[0:HALO, :] = jnp.zeros((HALO, D_FF), _F32)

    @pl.when(pl.program_id(1) > 0)
    def _():
        a_ext[0:HALO, :] = a_ext[tb:tb + HALO, :]

    hn_parts, gate_parts = [], []
    for r0 in range(0, tb, tb // 4):
        h = h_ref[0, r0:r0 + tb // 4, :]
        hn = h * lax.rsqrt(jnp.mean(h * h, axis=-1, keepdims=True) + EPS) * nw_ref[...]
        hn_parts.append(hn.astype(_BF16))
        a_ext[HALO + r0:HALO + r0 + tb // 4, 0:FF_TILE] = _dot(hn_parts[-1],
                                                               wup_ref[:, 0:FF_TILE])
        gate_parts.append(_dot(hn_parts[-1], wup_ref[:, D_FF:D_FF + FF_TILE]))
    hn = jnp.concatenate(hn_parts, axis=0)
    for c0 in range(0, D_FF, FF_TILE):
        cs = slice(c0, c0 + FF_TILE)
        if c0:
            a_ext[HALO:HALO + tb, cs] = _dot(hn, wup_ref[:, cs])
            gate = _dot(hn, wup_ref[:, D_FF + c0:D_FF + c0 + FF_TILE])
        else:
            gate = jnp.concatenate(gate_parts, axis=0)
        u = (cb_ref[:, cs]
             + a_ext[HALO - 2:HALO - 2 + tb, cs] * cw_ref[0:1, cs]
             + a_ext[HALO - 1:HALO - 1 + tb, cs] * cw_ref[1:2, cs]
             + a_ext[HALO:HALO + tb, cs] * cw_ref[2:3, cs])
        u_scr[:, cs] = (u * _sigmoid(u) * gate).astype(_BF16)
    for r0 in range(0, tb, tb // 4):
        rows = slice(r0, r0 + tb // 4)
        ssq = None
        for c0 in range(0, D_MODEL, FF_TILE):
            cs = slice(c0, c0 + FF_TILE)
            y = h_ref[0, rows, cs] + _dot(u_scr[rows, :], wdn_ref[:, cs])
            o_ref[0, rows, cs] = y
            part = jnp.sum(y * y, axis=-1, keepdims=True)
            ssq = part if ssq is None else ssq + part
        r = lax.rsqrt(ssq * (1.0 / D_MODEL) + EPS)
        o_ref[0, rows, :] = o_ref[0, rows, :] * r * fw_ref[...]


def _const_spec(shape):
    zeros = (0,) * len(shape)
    return pl.BlockSpec(shape, lambda b, t: zeros, pipeline_mode=pl.Buffered(1))


def kernel(x, attn_norm_w, w_in, hgrn_lower_bounds, w_gla_gate_up, b_gla_gate, hg_norm_w,
           gla_norm_w, w_out, ffn_norm_w, w_up, conv_w, conv_b, w_down, final_norm_w):
    batch, seq, d = x.shape
    assert d == D_MODEL and seq % MIXER_BLOCK == 0 and seq % FFN_BLOCK == 0
    assert w_in.shape[0] == 1, "single-layer problem"

    assert w_in.shape[2] == _GD + GLA_GATE_RANK
    win_t = jnp.transpose(w_in[0])
    wgu = w_gla_gate_up[0]
    wgu_hi = wgu.astype(_BF16)
    wgu_lo = (wgu - wgu_hi.astype(_F32)).astype(_BF16)
    head_nw = jnp.concatenate([hg_norm_w[0], gla_norm_w[0]])[None, :]
    n2_np, mask_np = _level_matrices(CHUNK)
    n2 = jnp.asarray(n2_np, dtype=_BF16)
    masks = jnp.asarray(mask_np)

    params = pltpu.CompilerParams(dimension_semantics=("arbitrary", "arbitrary"),
                                  vmem_limit_bytes=VMEM_LIMIT)

    def act_spec(block):
        return pl.BlockSpec((1, block, D_MODEL), lambda b, t: (b, t, 0))

    steps_per_row = seq // MIXER_BLOCK
    n_steps = batch * steps_per_row
    assert D_MODEL % n_steps == 0 and D_FF % WDN_SLAB == 0 and D_FF // WDN_SLAB <= n_steps
    wup_rows = D_MODEL // n_steps
    last_wdn = D_FF // WDN_SLAB - 1

    def step(b, t):
        return b * steps_per_row + t

    h, w_up_bf16, w_down_bf16 = pl.pallas_call(
        _mixer_kernel,
        grid=(batch, steps_per_row),
        in_specs=[
            act_spec(MIXER_BLOCK),
            _const_spec((1, D_MODEL)),
            _const_spec(win_t.shape),
            _const_spec(hgrn_lower_bounds.shape),
            _const_spec((GLA_GATE_RANK, GLA_KEY_WIDTH)),
            _const_spec((GLA_GATE_RANK, GLA_KEY_WIDTH)),
            _const_spec((1, GLA_KEY_WIDTH)),
            _const_spec((1, MIX)),
            _const_spec(w_out.shape),
            _const_spec(n2.shape),
            _const_spec(masks.shape),
            pl.BlockSpec((1, wup_rows, 2 * D_FF), lambda b, t: (0, step(b, t), 0)),
            pl.BlockSpec((1, WDN_SLAB, D_MODEL),
                         lambda b, t: (0, jnp.minimum(step(b, t), last_wdn), 0)),
        ],
        out_specs=[
            act_spec(MIXER_BLOCK),
            pl.BlockSpec((wup_rows, 2 * D_FF), lambda b, t: (step(b, t), 0)),
            pl.BlockSpec((WDN_SLAB, D_MODEL), lambda b, t: (jnp.minimum(step(b, t), last_wdn), 0)),
        ],
        out_shape=[
            jax.ShapeDtypeStruct(x.shape, _F32),
            jax.ShapeDtypeStruct((D_MODEL, 2 * D_FF), _BF16),
            jax.ShapeDtypeStruct((D_FF, D_MODEL), _BF16),
        ],
        scratch_shapes=[
            pltpu.VMEM((MIXER_BLOCK, PROJ_W), _F32),
            *[pltpu.VMEM((MIXER_BLOCK // CHUNK, n2.shape[0] + CHUNK, SLAB), _F32)
              for _ in range(N_SLABS)],
            pltpu.VMEM((MIXER_BLOCK // CHUNK, CHUNK, HG_WIDTH), _F32),
            pltpu.VMEM((MIXER_BLOCK // CHUNK, CHUNK, KEYS), _BF16),
            pltpu.VMEM((MIXER_BLOCK // CHUNK, CHUNK, KEYS), _BF16),
            pltpu.VMEM((MIXER_BLOCK // CHUNK, SUBLANES, KEYS), _F32),
            pltpu.VMEM((MIXER_BLOCK, MIX), _F32),
            pltpu.VMEM((N_HEADS, CHUNK, CHUNK), _BF16),
            pltpu.VMEM((N_HEADS, HEAD, LANE), _F32),
            pltpu.VMEM((MIXER_BLOCK, MIX), _BF16),
        ],
        compiler_params=params,
        name="mixer",
    )(x, attn_norm_w, win_t, hgrn_lower_bounds, wgu_hi, wgu_lo, b_gla_gate, head_nw,
      w_out, n2, masks, w_up, w_down)

    out = pl.pallas_call(
        _ffn_kernel,
        grid=(batch, seq // FFN_BLOCK),
        in_specs=[
            act_spec(FFN_BLOCK),
            _const_spec((1, D_MODEL)),
            _const_spec((D_MODEL, 2 * D_FF)),
            _const_spec((3, D_FF)),
            _const_spec((1, D_FF)),
            _const_spec((D_FF, D_MODEL)),
            _const_spec((1, D_MODEL)),
        ],
        out_specs=act_spec(FFN_BLOCK),
        out_shape=jax.ShapeDtypeStruct(x.shape, _F32),
        scratch_shapes=[
            pltpu.VMEM((HALO + FFN_BLOCK, D_FF), _F32),
            pltpu.VMEM((FFN_BLOCK, D_FF), _BF16),
        ],
        compiler_params=params,
        name="ffn",
    )(h, ffn_norm_w, w_up_bf16, conv_w[0], conv_b, w_down_bf16,
      final_norm_w[None, :])
    return out
```

```python
import numpy as np
import jax
import jax.numpy as jnp
from jax import lax
from jax.experimental import pallas as pl
from jax.experimental.pallas import tpu as pltpu

D_MODEL = 1024
HG_WIDTH = 512
GLA_WIDTH = 512
GLA_DK = 64
GLA_KEY_WIDTH = 256
GLA_GATE_RANK = 16
GLA_GATE_TAU = 16.0
D_FF = 2816
EPS = 1e-6
LOG2_E = 1.4426950408889634

LANE = 128
HEAD = 128
N_HEADS = 8
MIX = N_HEADS * HEAD
SLAB = 2 * LANE
N_SLABS = 3
KEYS = N_SLABS * SLAB
CHUNK = 128
MIXER_BLOCK = 512
FFN_BLOCK = 1024
FF_TILE = 256
WDN_SLAB = 128
SUBLANES = 8
BF16_ROWS = 2 * SUBLANES
HALO = SUBLANES
VMEM_LIMIT = 58 * 1024 * 1024

_HQ, _HF, _HI, _HGT = 0, 512, 1024, 1536
_GQ, _GK, _GV, _GG, _GD = 2048, 2304, 2560, 3072, 3584
PROJ_W = _GD + LANE

_F32 = jnp.float32
_BF16 = jnp.bfloat16


_SCORE_TERMS = ((0, 0), (1, 1)) + tuple((lv, None) for lv in range(2, CHUNK.bit_length()))
_BLOCK_DOT_MIN = CHUNK // 2


def _level_matrices(c):
    r = np.arange(c)[:, None]
    s = np.arange(c)[None, :]
    blocks = [s <= r]
    level_masks = [np.eye(c, dtype=bool)]
    m = 1
    while m < c:
        ref = (r // (2 * m)) * (2 * m) + m - 1
        hi = (r // m) % 2 == 1
        if 1 < m < SUBLANES:
            blocks.append(np.where(hi, (s > ref) & (s <= r), (s > r) & (s <= ref)))
        level_masks.append(((r // (2 * m)) == (s // (2 * m))) & hi & ((s // m) % 2 == 0))
        m *= 2
    masks = [level_masks[lv] if delta is None else level_masks[lv] & (r - s == delta)
             for lv, delta in _SCORE_TERMS]
    assert (np.sum(masks, axis=0) == (s <= r)).all()
    n = np.concatenate(blocks, axis=0).astype(np.float32)
    n2 = np.concatenate([n, n], axis=-1)
    return n2, np.stack(masks).astype(np.float32)


def _sigmoid(x):
    return 1.0 / (1.0 + jnp.exp(-x))


def _split_bf16(x):
    hi = x.astype(_BF16)
    lo = (x - hi.astype(_F32)).astype(_BF16)
    return hi, lo


def _dot(a, b):
    return jnp.dot(a, b, preferred_element_type=_F32)


def _dot_nt(a, b):
    return lax.dot_general(a, b, (((1,), (1,)), ((), ())), preferred_element_type=_F32)


def _mixer_kernel(x_ref, nw_ref, win_ref, lb_ref, wgh_ref, wgl_ref, bg_ref, hnw_ref,
                  wout_ref, n2_ref, mask_ref, wup_ref, wdn_ref, o_ref, wup_o_ref, wdn_o_ref,
                  proj_scr, xs0, xs1, xs2, k_all, qd_scr, kd_scr, cd_scr, g_scr, a_scr, st_scr,
                  o_scr):

    @pl.when(pl.program_id(1) == 0)
    def _():
        st_scr[...] = jnp.zeros_like(st_scr)

    x = x_ref[0]
    xn = x * lax.rsqrt(jnp.mean(x * x, axis=-1, keepdims=True) + EPS) * nw_ref[...]
    xn = xn.astype(_BF16)

    def project(c0, width):
        proj_scr[:, c0:c0 + width] = _dot_nt(xn, win_ref[c0:c0 + width, :].astype(_BF16))

    p = lb_ref[...]
    pe = jnp.exp(p - jnp.max(p, axis=0, keepdims=True))
    lb = pe[0:1, :] / jnp.sum(pe, axis=0, keepdims=True)

    lane = lax.broadcasted_iota(jnp.int32, (1, LANE), 1)
    half_f = [(lane < GLA_DK).astype(_F32), (lane >= GLA_DK).astype(_F32)]
    half = [hf.astype(_BF16) for hf in half_f]
    n_mxu_rows = n2_ref.shape[0]

    def chunk_body(ci, phase):
        rows = slice(ci * CHUNK, (ci + 1) * CHUNK)
        x_scrs, k_scr = [xs.at[ci] for xs in (xs0, xs1, xs2)], k_all.at[ci]
        slabs = {p: range(N_SLABS) if p == phase else () for p in (1, 2, 3, 4)}

        def load_qk(s):
            if s < 2:
                c0 = s * SLAB
                return proj_scr[rows, _HQ + c0:_HQ + c0 + SLAB], k_scr[:, c0:c0 + SLAB]
            q = proj_scr[rows, _GQ:_GQ + SLAB] * (GLA_DK ** -0.5)
            return q, proj_scr[rows, _GK:_GK + SLAB]

        for s in slabs[1]:
            if s < 2:
                c0 = s * SLAB
                lb_s = lb[:, c0:c0 + SLAB]
                f = lb_s + (1.0 - lb_s) * _sigmoid(proj_scr[rows, _HF + c0:_HF + c0 + SLAB])
                g = jnp.log2(f)
                k_scr[:, c0:c0 + SLAB] = 1.0 - f
            else:
                gd_hi, gd_lo = _split_bf16(proj_scr[rows, _GD:_GD + GLA_GATE_RANK])
                logit = (_dot(gd_hi, wgh_ref[...]) + _dot(gd_lo, wgh_ref[...])
                         + _dot(gd_hi, wgl_ref[...]) + bg_ref[...])
                g = (jnp.minimum(logit, 0.0) * LOG2_E
                     - jnp.log2(1.0 + jnp.exp(-jnp.abs(logit)))) * (1.0 / GLA_GATE_TAU)
            g_hi, g_lo = _split_bf16(g)
            x_scrs[s][0:n_mxu_rows, :] = _dot(n2_ref[...], jnp.concatenate([g_hi, g_lo], axis=0))
            x_scrs[s][n_mxu_rows:n_mxu_rows + CHUNK, :] = g

        def level_exponent(s, lv):
            m = 1 << (lv - 1)
            if m < SUBLANES:
                return x_scrs[s][(lv - 1) * CHUNK:lv * CHUNK, :]
            cum = x_scrs[s][0:CHUNK, :]
            ref = jnp.concatenate(
                [jnp.broadcast_to(cum[r:r + 1, :], (2 * m, SLAB))
                 for r in range(m - 1, CHUNK, 2 * m)], axis=0)
            return -jnp.abs(cum - ref)

        for s in slabs[2]:
            q_f, k_f = load_qk(s)
            q, k = q_f.astype(_BF16), k_f.astype(_BF16)
            n_pieces = CHUNK // SUBLANES
            for j in range(2):
                js = slice(j * LANE, (j + 1) * LANE)
                heads = 1 if s < 2 else 2
                acc = [[None] * n_pieces for _ in range(heads)]
                for mi, (lv, delta) in enumerate(_SCORE_TERMS):
                    blk = 1 << max(lv - 1, 0)
                    if lv == 0 or blk < SUBLANES:
                        live = list(range(n_pieces))
                    else:
                        live = [p for p in range(n_pieces) if (p * SUBLANES // blk) % 2 == 1]
                    lhs_rows = list(range(n_pieces))
                    if delta is not None:
                        if lv == 0:
                            prod = q_f[:, js] * k_f[:, js]
                        else:
                            g_own = x_scrs[s][n_mxu_rows:n_mxu_rows + CHUNK, js]
                            prod = (q_f[:, js] * jnp.exp2(g_own)
                                    * pltpu.roll(k_f[:, js], delta, 0))
                        if s < 2:
                            parts = [jnp.sum(prod, axis=-1, keepdims=True)]
                        else:
                            parts = [jnp.sum(prod * half_f[i], axis=-1, keepdims=True)
                                     for i in range(2)]
                    else:
                        e = jnp.exp2(level_exponent(s, lv)[:, js]).astype(_BF16)
                        qp, kp = q[:, js] * e, k[:, js] * e
                        if blk >= _BLOCK_DOT_MIN:
                            for r0 in range(0, CHUNK, 2 * blk):
                                q_b, k_b = qp[r0 + blk:r0 + 2 * blk], kp[r0:r0 + blk]
                                if s >= 2:
                                    q_b = jnp.concatenate([q_b * half[0], q_b * half[1]], axis=0)
                                r = _dot_nt(q_b, k_b)
                                r = jnp.concatenate(
                                    [r, jnp.zeros((r.shape[0], CHUNK - blk), _F32)], axis=1)
                                if r0:
                                    r = pltpu.roll(r, r0, 1)
                                for i in range(heads):
                                    for t in range(blk // SUBLANES):
                                        p = (r0 + blk) // SUBLANES + t
                                        src = i * blk + t * SUBLANES
                                        a = r[src:src + SUBLANES]
                                        acc[i][p] = a if acc[i][p] is None else acc[i][p] + a
                            continue
                        if blk >= BF16_ROWS:
                            lhs_rows = live
                            qp = jnp.concatenate(
                                [qp[r:r + blk] for r in range(blk, CHUNK, 2 * blk)], axis=0)
                        n_lhs = qp.shape[0]
                        if s < 2:
                            parts = [_dot_nt(qp, kp)]
                        else:
                            r = _dot_nt(jnp.concatenate([qp * half[0], qp * half[1]], axis=0), kp)
                            parts = [r[0:n_lhs], r[n_lhs:2 * n_lhs]]
                    for i, part in enumerate(parts):
                        for p in live:
                            src = lhs_rows.index(p) * SUBLANES
                            a = (part[src:src + SUBLANES]
                                 * mask_ref[mi, p * SUBLANES:(p + 1) * SUBLANES, :])
                            acc[i][p] = a if acc[i][p] is None else acc[i][p] + a
                for i in range(heads):
                    a_scr[(2 * s if s < 2 else 4) + j * heads + i] = jnp.concatenate(
                        acc[i], axis=0).astype(_BF16)

        for s in slabs[4]:
            q, k = load_qk(s)
            ks = slice(s * SLAB, (s + 1) * SLAB)
            cum = x_scrs[s][0:CHUNK, :]
            e_q = jnp.exp2(cum)
            e_k = jnp.exp2(cum[CHUNK - 1:CHUNK, :] - cum)
            qd_scr[ci, :, ks] = (q * e_q).astype(_BF16)
            kd_scr[ci, :, ks] = (k * e_k).astype(_BF16)
            cd_scr[ci, 0:1, ks] = e_q[CHUNK - 1:CHUNK, :]

        for s in slabs[3]:
            ks = slice(s * SLAB, (s + 1) * SLAB)
            qd, kd = qd_scr[ci, :, ks], kd_scr[ci, :, ks]
            chunk_decay = cd_scr[ci, 0:1, ks]
            for idx in range(2 if s < 2 else 4):
                if s < 2:
                    h = 2 * s + idx
                    js = slice(idx * LANE, (idx + 1) * LANE)
                    qd_h, kd_h = qd[:, js], kd[:, js]
                    v_col = _HI + h * HEAD
                else:
                    h = 4 + idx
                    js = slice((idx // 2) * LANE, (idx // 2 + 1) * LANE)
                    qd_h, kd_h = qd[:, js] * half[idx % 2], kd[:, js] * half[idx % 2]
                    v_col = _GV + idx * HEAD
                hs = slice(h * HEAD, (h + 1) * HEAD)
                st = st_scr[h]
                v_f = proj_scr[rows, v_col:v_col + HEAD]
                o_h = _dot(a_scr[h], v_f.astype(_BF16)) + _dot(qd_h, st.T.astype(_BF16))
                st_scr[h] = st * chunk_decay[:, js] + _dot(v_f.T.astype(_BF16), kd_h)
                o_h = o_h * lax.rsqrt(jnp.mean(o_h * o_h, axis=-1, keepdims=True) + EPS)
                o_scr[rows, hs] = (o_h * g_scr[rows, hs]).astype(_BF16)

    def output_gates(col, out_col):
        gate = proj_scr[:, col:col + HG_WIDTH]
        g_scr[:, out_col:out_col + HG_WIDTH] = (gate * _sigmoid(gate)
                                                * hnw_ref[:, out_col:out_col + HG_WIDTH])

    n_chunks = MIXER_BLOCK // CHUNK
    assert n_chunks == 4
    project(_HF, HG_WIDTH)
    wup_o_ref[...] = wup_ref[0].astype(_BF16)
    wdn_o_ref[...] = wdn_ref[0].astype(_BF16)
    project(_GD, GLA_GATE_RANK)
    plan = [(_HQ, [(chunk_body, 0, 1)]),
            (_GQ, [(chunk_body, 1, 1)]),
            (_HGT, [(chunk_body, 2, 1)]),
            (_GG, [(chunk_body, 3, 1)]),
            (_HI, [(chunk_body, 0, 4), (chunk_body, 1, 4), (output_gates, _HGT, 0)]),
            (_GV, [(chunk_body, 2, 4), (chunk_body, 3, 4), (output_gates, _GG, HG_WIDTH)])]
    for c0, overlapped in plan:
        project(c0, HG_WIDTH)
        for fn, a, b in overlapped:
            fn(a, b)
    for ci in range(n_chunks):
        chunk_body(ci, 2)
        chunk_body(ci, 3)

    o_ref[0] = x + _dot(o_scr[...], wout_ref[0].astype(_BF16))


def _ffn_kernel(h_ref, nw_ref, wup_ref, cw_ref, cb_ref, wdn_ref, fw_ref, o_ref,
                a_ext, u_scr):
    tb = h_ref.shape[1]

    @pl.when(pl.program_id(1) == 0)
    def _():
        a_ext[0:HALO, :] = jnp.zeros((HALO, D_FF), _F32)

    @pl.when(pl.program_id(1) > 0)
    def _():
        a_ext[0:HALO, :] = a_ext[tb:tb + HALO, :]

    hn_parts, gate_parts = [], []
    for r0 in range(0, tb, tb // 4):
        h = h_ref[0, r0:r0 + tb // 4, :]
        hn = h * lax.rsqrt(jnp.mean(h * h, axis=-1, keepdims=True) + EPS) * nw_ref[...]
        hn_parts.append(hn.astype(_BF16))
        a_ext[HALO + r0:HALO + r0 + tb // 4, 0:FF_TILE] = _dot(hn_parts[-1],
                                                               wup_ref[:, 0:FF_TILE])
        gate_parts.append(_dot(hn_parts[-1], wup_ref[:, D_FF:D_FF + FF_TILE]))
    hn = jnp.concatenate(hn_parts, axis=0)
    for c0 in range(0, D_FF, FF_TILE):
        cs = slice(c0, c0 + FF_TILE)
        if c0:
            a_ext[HALO:HALO + tb, cs] = _dot(hn, wup_ref[:, cs])
            gate = _dot(hn, wup_ref[:, D_FF + c0:D_FF + c0 + FF_TILE])
        else:
            gate = jnp.concatenate(gate_parts, axis=0)
        u = (cb_ref[:, cs]
             + a_ext[HALO - 2:HALO - 2 + tb, cs] * cw_ref[0:1, cs]
             + a_ext[HALO - 1:HALO - 1 + tb, cs] * cw_ref[1:2, cs]
             + a_ext[HALO:HALO + tb, cs] * cw_ref[2:3, cs])
        u_scr[:, cs] = (u * _sigmoid(u) * gate).astype(_BF16)
    for r0 in range(0, tb, tb // 4):
        rows = slice(r0, r0 + tb // 4)
        y = h_ref[0, rows, :] + _dot(u_scr[rows, :], wdn_ref[...])
        o_ref[0, rows, :] = (y * lax.rsqrt(jnp.mean(y * y, axis=-1, keepdims=True) + EPS)
                             * fw_ref[...])


def _const_spec(shape):
    zeros = (0,) * len(shape)
    return pl.BlockSpec(shape, lambda b, t: zeros, pipeline_mode=pl.Buffered(1))


def kernel(x, attn_norm_w, w_in, hgrn_lower_bounds, w_gla_gate_up, b_gla_gate, hg_norm_w,
           gla_norm_w, w_out, ffn_norm_w, w_up, conv_w, conv_b, w_down, final_norm_w):
    batch, seq, d = x.shape
    assert d == D_MODEL and seq % MIXER_BLOCK == 0 and seq % FFN_BLOCK == 0
    assert w_in.shape[0] == 1, "single-layer problem"

    assert w_in.shape[2] == _GD + GLA_GATE_RANK
    win_t = jnp.transpose(w_in[0])
    wgu = w_gla_gate_up[0]
    wgu_hi = wgu.astype(_BF16)
    wgu_lo = (wgu - wgu_hi.astype(_F32)).astype(_BF16)
    head_nw = jnp.concatenate([hg_norm_w[0], gla_norm_w[0]])[None, :]
    n2_np, mask_np = _level_matrices(CHUNK)
    n2 = jnp.asarray(n2_np, dtype=_BF16)
    masks = jnp.asarray(mask_np)

    params = pltpu.CompilerParams(dimension_semantics=("arbitrary", "arbitrary"),
                                  vmem_limit_bytes=VMEM_LIMIT)

    def act_spec(block):
        return pl.BlockSpec((1, block, D_MODEL), lambda b, t: (b, t, 0))

    steps_per_row = seq // MIXER_BLOCK
    n_steps = batch * steps_per_row
    assert D_MODEL % n_steps == 0 and D_FF % WDN_SLAB == 0 and D_FF // WDN_SLAB <= n_steps
    wup_rows = D_MODEL // n_steps
    last_wdn = D_FF // WDN_SLAB - 1

    def step(b, t):
        return b * steps_per_row + t

    h, w_up_bf16, w_down_bf16 = pl.pallas_call(
        _mixer_kernel,
        grid=(batch, steps_per_row),
        in_specs=[
            act_spec(MIXER_BLOCK),
            _const_spec((1, D_MODEL)),
            _const_spec(win_t.shape),
            _const_spec(hgrn_lower_bounds.shape),
            _const_spec((GLA_GATE_RANK, GLA_KEY_WIDTH)),
            _const_spec((GLA_GATE_RANK, GLA_KEY_WIDTH)),
            _const_spec((1, GLA_KEY_WIDTH)),
            _const_spec((1, MIX)),
            _const_spec(w_out.shape),
            _const_spec(n2.shape),
            _const_spec(masks.shape),
            pl.BlockSpec((1, wup_rows, 2 * D_FF), lambda b, t: (0, step(b, t), 0)),
            pl.BlockSpec((1, WDN_SLAB, D_MODEL),
                         lambda b, t: (0, jnp.minimum(step(b, t), last_wdn), 0)),
        ],
        out_specs=[
            act_spec(MIXER_BLOCK),
            pl.BlockSpec((wup_rows, 2 * D_FF), lambda b, t: (step(b, t), 0)),
            pl.BlockSpec((WDN_SLAB, D_MODEL), lambda b, t: (jnp.minimum(step(b, t), last_wdn), 0)),
        ],
        out_shape=[
            jax.ShapeDtypeStruct(x.shape, _F32),
            jax.ShapeDtypeStruct((D_MODEL, 2 * D_FF), _BF16),
            jax.ShapeDtypeStruct((D_FF, D_MODEL), _BF16),
        ],
        scratch_shapes=[
            pltpu.VMEM((MIXER_BLOCK, PROJ_W), _F32),
            *[pltpu.VMEM((MIXER_BLOCK // CHUNK, n2.shape[0] + CHUNK, SLAB), _F32)
              for _ in range(N_SLABS)],
            pltpu.VMEM((MIXER_BLOCK // CHUNK, CHUNK, HG_WIDTH), _F32),
            pltpu.VMEM((MIXER_BLOCK // CHUNK, CHUNK, KEYS), _BF16),
            pltpu.VMEM((MIXER_BLOCK // CHUNK, CHUNK, KEYS), _BF16),
            pltpu.VMEM((MIXER_BLOCK // CHUNK, SUBLANES, KEYS), _F32),
            pltpu.VMEM((MIXER_BLOCK, MIX), _F32),
            pltpu.VMEM((N_HEADS, CHUNK, CHUNK), _BF16),
            pltpu.VMEM((N_HEADS, HEAD, LANE), _F32),
            pltpu.VMEM((MIXER_BLOCK, MIX), _BF16),
        ],
        compiler_params=params,
        name="mixer",
    )(x, attn_norm_w, win_t, hgrn_lower_bounds, wgu_hi, wgu_lo, b_gla_gate, head_nw,
      w_out, n2, masks, w_up, w_down)

    out = pl.pallas_call(
        _ffn_kernel,
        grid=(batch, seq // FFN_BLOCK),
        in_specs=[
            act_spec(FFN_BLOCK),
            _const_spec((1, D_MODEL)),
            _const_spec((D_MODEL, 2 * D_FF)),
            _const_spec((3, D_FF)),
            _const_spec((1, D_FF)),
            _const_spec((D_FF, D_MODEL)),
            _const_spec((1, D_MODEL)),
        ],
        out_specs=act_spec(FFN_BLOCK),
        out_shape=jax.ShapeDtypeStruct(x.shape, _F32),
        scratch_shapes=[
            pltpu.VMEM((HALO + FFN_BLOCK, D_FF), _F32),
            pltpu.VMEM((FFN_BLOCK, D_FF), _BF16),
        ],
        compiler_params=params,
        name="ffn",
    )(h, ffn_norm_w, w_up_bf16, conv_w[0], conv_b, w_down_bf16,
      final_norm_w[None, :])
    return out
```

```python
import numpy as np
import jax
import jax.numpy as jnp
from jax import lax
from jax.experimental import pallas as pl
from jax.experimental.pallas import tpu as pltpu

D_MODEL = 1024
HG_WIDTH = 512
GLA_WIDTH = 512
GLA_DK = 64
GLA_KEY_WIDTH = 256
GLA_GATE_RANK = 16
GLA_GATE_TAU = 16.0
D_FF = 2816
EPS = 1e-6
LOG2_E = 1.4426950408889634

LANE = 128
HEAD = 128
N_HEADS = 8
MIX = N_HEADS * HEAD
SLAB = 2 * LANE
N_SLABS = 3
KEYS = N_SLABS * SLAB
CHUNK = 128
MIXER_BLOCK = 512
FFN_BLOCK = 1024
FF_TILE = 256
WDN_SLAB = 128
SUBLANES = 8
BF16_ROWS = 2 * SUBLANES
HALO = SUBLANES
VMEM_LIMIT = 58 * 1024 * 1024

_HQ, _HF, _HI, _HGT = 0, 512, 1024, 1536
_GQ, _GK, _GV, _GG, _GD = 2048, 2304, 2560, 3072, 3584
PROJ_W = _GD + LANE

_F32 = jnp.float32
_BF16 = jnp.bfloat16


_SCORE_TERMS = ((0, 0), (1, 1)) + tuple((lv, None) for lv in range(2, CHUNK.bit_length()))
_BLOCK_DOT_MIN = CHUNK // 2


def _level_matrices(c):
    r = np.arange(c)[:, None]
    s = np.arange(c)[None, :]
    blocks = [s <= r]
    level_masks = [np.eye(c, dtype=bool)]
    m = 1
    while m < c:
        ref = (r // (2 * m)) * (2 * m) + m - 1
        hi = (r // m) % 2 == 1
        if 1 < m < SUBLANES:
            blocks.append(np.where(hi, (s > ref) & (s <= r), (s > r) & (s <= ref)))
        level_masks.append(((r // (2 * m)) == (s // (2 * m))) & hi & ((s // m) % 2 == 0))
        m *= 2
    masks = [level_masks[lv] if delta is None else level_masks[lv] & (r - s == delta)
             for lv, delta in _SCORE_TERMS]
    assert (np.sum(masks, axis=0) == (s <= r)).all()
    n = np.concatenate(blocks, axis=0).astype(np.float32)
    n2 = np.concatenate([n, n], axis=-1)
    return n2, np.stack(masks).astype(np.float32)


def _sigmoid(x):
    return 1.0 / (1.0 + jnp.exp(-x))


def _split_bf16(x):
    hi = x.astype(_BF16)
    lo = (x - hi.astype(_F32)).astype(_BF16)
    return hi, lo


def _dot(a, b):
    return jnp.dot(a, b, preferred_element_type=_F32)


def _dot_nt(a, b):
    return lax.dot_general(a, b, (((1,), (1,)), ((), ())), preferred_element_type=_F32)


def _mixer_kernel(x_ref, nw_ref, win_ref, lb_ref, wgh_ref, wgl_ref, bg_ref, hnw_ref,
                  wout_ref, n2_ref, mask_ref, wup_ref, wdn_ref, o_ref, wup_o_ref, wdn_o_ref,
                  proj_scr, xs0, xs1, xs2, k_all, qd_scr, kd_scr, cd_scr, g_scr, a_scr, st_scr,
                  o_scr):

    @pl.when(pl.program_id(1) == 0)
    def _():
        st_scr[...] = jnp.zeros_like(st_scr)

    x = x_ref[0]
    xn = x * lax.rsqrt(jnp.mean(x * x, axis=-1, keepdims=True) + EPS) * nw_ref[...]
    xn = xn.astype(_BF16)

    def project(c0, width):
        proj_scr[:, c0:c0 + width] = _dot_nt(xn, win_ref[c0:c0 + width, :].astype(_BF16))

    p = lb_ref[...]
    pe = jnp.exp(p - jnp.max(p, axis=0, keepdims=True))
    lb = pe[0:1, :] / jnp.sum(pe, axis=0, keepdims=True)

    lane = lax.broadcasted_iota(jnp.int32, (1, LANE), 1)
    half_f = [(lane < GLA_DK).astype(_F32), (lane >= GLA_DK).astype(_F32)]
    half = [hf.astype(_BF16) for hf in half_f]
    n_mxu_rows = n2_ref.shape[0]

    def chunk_body(ci, phase):
        rows = slice(ci * CHUNK, (ci + 1) * CHUNK)
        x_scrs, k_scr = [xs.at[ci] for xs in (xs0, xs1, xs2)], k_all.at[ci]
        slabs = {p: range(N_SLABS) if p == phase else () for p in (1, 2, 3, 4)}

        def load_qk(s):
            if s < 2:
                c0 = s * SLAB
                return proj_scr[rows, _HQ + c0:_HQ + c0 + SLAB], k_scr[:, c0:c0 + SLAB]
            q = proj_scr[rows, _GQ:_GQ + SLAB] * (GLA_DK ** -0.5)
            return q, proj_scr[rows, _GK:_GK + SLAB]

        for s in slabs[1]:
            if s < 2:
                c0 = s * SLAB
                lb_s = lb[:, c0:c0 + SLAB]
                f = lb_s + (1.0 - lb_s) * _sigmoid(proj_scr[rows, _HF + c0:_HF + c0 + SLAB])
                g = jnp.log2(f)
                k_scr[:, c0:c0 + SLAB] = 1.0 - f
            else:
                gd_hi, gd_lo = _split_bf16(proj_scr[rows, _GD:_GD + GLA_GATE_RANK])
                logit = (_dot(gd_hi, wgh_ref[...]) + _dot(gd_lo, wgh_ref[...])
                         + _dot(gd_hi, wgl_ref[...]) + bg_ref[...])
                g = (jnp.minimum(logit, 0.0) * LOG2_E
                     - jnp.log2(1.0 + jnp.exp(-jnp.abs(logit)))) * (1.0 / GLA_GATE_TAU)
            g_hi, g_lo = _split_bf16(g)
            x_scrs[s][0:n_mxu_rows, :] = _dot(n2_ref[...], jnp.concatenate([g_hi, g_lo], axis=0))
            x_scrs[s][n_mxu_rows:n_mxu_rows + CHUNK, :] = g

        def level_exponent(s, lv):
            m = 1 << (lv - 1)
            if m < SUBLANES:
                return x_scrs[s][(lv - 1) * CHUNK:lv * CHUNK, :]
            cum = x_scrs[s][0:CHUNK, :]
            ref = jnp.concatenate(
                [jnp.broadcast_to(cum[r:r + 1, :], (2 * m, SLAB))
                 for r in range(m - 1, CHUNK, 2 * m)], axis=0)
            return -jnp.abs(cum - ref)

        for s in slabs[2]:
            q_f, k_f = load_qk(s)
            q, k = q_f.astype(_BF16), k_f.astype(_BF16)
            n_pieces = CHUNK // SUBLANES
            for j in range(2):
                js = slice(j * LANE, (j + 1) * LANE)
                heads = 1 if s < 2 else 2
                acc = [[None] * n_pieces for _ in range(heads)]
                for mi, (lv, delta) in enumerate(_SCORE_TERMS):
                    blk = 1 << max(lv - 1, 0)
                    if lv == 0 or blk < SUBLANES:
                        live = list(range(n_pieces))
                    else:
                        live = [p for p in range(n_pieces) if (p * SUBLANES // blk) % 2 == 1]
                    lhs_rows = list(range(n_pieces))
                    if delta is not None:
                        if lv == 0:
                            prod = q_f[:, js] * k_f[:, js]
                        else:
                            g_own = x_scrs[s][n_mxu_rows:n_mxu_rows + CHUNK, js]
                            prod = (q_f[:, js] * jnp.exp2(g_own)
                                    * pltpu.roll(k_f[:, js], delta, 0))
                        if s < 2:
                            parts = [jnp.sum(prod, axis=-1, keepdims=True)]
                        else:
                            parts = [jnp.sum(prod * half_f[i], axis=-1, keepdims=True)
                                     for i in range(2)]
                    else:
                        e = jnp.exp2(level_exponent(s, lv)[:, js]).astype(_BF16)
                        qp, kp = q[:, js] * e, k[:, js] * e
                        if blk >= _BLOCK_DOT_MIN:
                            for r0 in range(0, CHUNK, 2 * blk):
                                q_b, k_b = qp[r0 + blk:r0 + 2 * blk], kp[r0:r0 + blk]
                                if s >= 2:
                                    q_b = jnp.concatenate([q_b * half[0], q_b * half[1]], axis=0)
                                r = _dot_nt(q_b, k_b)
                                r = jnp.concatenate(
                                    [r, jnp.zeros((r.shape[0], CHUNK - blk), _F32)], axis=1)
                                if r0:
                                    r = pltpu.roll(r, r0, 1)
                                for i in range(heads):
                                    for t in range(blk // SUBLANES):
                                        p = (r0 + blk) // SUBLANES + t
                                        src = i * blk + t * SUBLANES
                                        a = r[src:src + SUBLANES]
                                        acc[i][p] = a if acc[i][p] is None else acc[i][p] + a
                            continue
                        if blk >= BF16_ROWS:
                            lhs_rows = live
                            qp = jnp.concatenate(
                                [qp[r:r + blk] for r in range(blk, CHUNK, 2 * blk)], axis=0)
                        n_lhs = qp.shape[0]
                        if s < 2:
                            parts = [_dot_nt(qp, kp)]
                        else:
                            r = _dot_nt(jnp.concatenate([qp * half[0], qp * half[1]], axis=0), kp)
                            parts = [r[0:n_lhs], r[n_lhs:2 * n_lhs]]
                    for i, part in enumerate(parts):
                        for p in live:
                            src = lhs_rows.index(p) * SUBLANES
                            a = (part[src:src + SUBLANES]
                                 * mask_ref[mi, p * SUBLANES:(p + 1) * SUBLANES, :])
                            acc[i][p] = a if acc[i][p] is None else acc[i][p] + a
                for i in range(heads):
                    a_scr[(2 * s if s < 2 else 4) + j * heads + i] = jnp.concatenate(
                        acc[i], axis=0).astype(_BF16)

        for s in slabs[4]:
            q, k = load_qk(s)
            ks = slice(s * SLAB, (s + 1) * SLAB)
            cum = x_scrs[s][0:CHUNK, :]
            e_q = jnp.exp2(cum)
            e_k = jnp.exp2(cum[CHUNK - 1:CHUNK, :] - cum)
            qd_scr[ci, :, ks] = (q * e_q).astype(_BF16)
            kd_scr[ci, :, ks] = (k * e_k).astype(_BF16)
            cd_scr[ci, 0:1, ks] = e_q[CHUNK - 1:CHUNK, :]

        for s in slabs[3]:
            ks = slice(s * SLAB, (s + 1) * SLAB)
            qd, kd = qd_scr[ci, :, ks], kd_scr[ci, :, ks]
            chunk_decay = cd_scr[ci, 0:1, ks]
            for idx in range(2 if s < 2 else 4):
                if s < 2:
                    h = 2 * s + idx
                    js = slice(idx * LANE, (idx + 1) * LANE)
                    qd_h, kd_h = qd[:, js], kd[:, js]
                    v_col = _HI + h * HEAD
                else:
                    h = 4 + idx
                    js = slice((idx // 2) * LANE, (idx // 2 + 1) * LANE)
                    qd_h, kd_h = qd[:, js] * half[idx % 2], kd[:, js] * half[idx % 2]
                    v_col = _GV + idx * HEAD
                hs = slice(h * HEAD, (h + 1) * HEAD)
                st = st_scr[h]
                v_f = proj_scr[rows, v_col:v_col + HEAD]
                o_h = _dot(a_scr[h], v_f.astype(_BF16)) + _dot(qd_h, st.T.astype(_BF16))
                st_scr[h] = st * chunk_decay[:, js] + _dot(v_f.T.astype(_BF16), kd_h)
                o_h = o_h * lax.rsqrt(jnp.mean(o_h * o_h, axis=-1, keepdims=True) + EPS)
                o_scr[rows, hs] = (o_h * g_scr[rows, hs]).astype(_BF16)

    def output_gates(col, out_col):
        gate = proj_scr[:, col:col + HG_WIDTH]
        g_scr[:, out_col:out_col + HG_WIDTH] = (gate * _sigmoid(gate)
                                                * hnw_ref[:, out_col:out_col + HG_WIDTH])

    n_chunks = MIXER_BLOCK // CHUNK
    assert n_chunks == 4
    project(_HF, HG_WIDTH)
    wup_o_ref[...] = wup_ref[0].astype(_BF16)
    wdn_o_ref[...] = wdn_ref[0].astype(_BF16)
    project(_GD, GLA_GATE_RANK)
    plan = [(_HQ, [(chunk_body, 0, 1)]),
            (_GQ, [(chunk_body, 1, 1)]),
            (_HGT, [(chunk_body, 2, 1)]),
            (_GG, [(chunk_body, 3, 1)]),
            (_HI, [(chunk_body, 0, 4), (chunk_body, 1, 4), (output_gates, _HGT, 0)]),
            (_GV, [(chunk_body, 2, 4), (chunk_body, 3, 4), (output_gates, _GG, HG_WIDTH)])]
    for c0, overlapped in plan:
        project(c0, HG_WIDTH)
        for fn, a, b in overlapped:
            fn(a, b)
    for ci in range(n_chunks):
        chunk_body(ci, 2)
        chunk_body(ci, 3)

    o_ref[0] = x + _dot(o_scr[...], wout_ref[0].astype(_BF16))


def _ffn_kernel(h_ref, nw_ref, wup_ref, cw_ref, cb_ref, wdn_ref, fw_ref, o_ref,
                a_ext, u_scr, hn_scr):
    tb = h_ref.shape[1]

    @pl.when(pl.program_id(1) == 0)
    def _():
        a_ext[0:HALO, :] = jnp.zeros((HALO, D_FF), _F32)

    @pl.when(pl.program_id(1) > 0)
    def _():
        a_ext[0:HALO, :] = a_ext[tb:tb + HALO, :]

    hn_parts, gate_parts = [], []
    for r0 in range(0, tb, tb // 4):
        h = h_ref[0, r0:r0 + tb // 4, :]
        hn = h * lax.rsqrt(jnp.mean(h * h, axis=-1, keepdims=True) + EPS) * nw_ref[...]
        hn_parts.append(hn.astype(_BF16))
        hn_scr[r0:r0 + tb // 4, :] = hn_parts[-1]
        a_ext[HALO + r0:HALO + r0 + tb // 4, 0:FF_TILE] = _dot(hn_parts[-1],
                                                               wup_ref[:, 0:FF_TILE])
        gate_parts.append(_dot(hn_parts[-1], wup_ref[:, D_FF:D_FF + FF_TILE]))
    for c0 in range(0, D_FF, FF_TILE):
        cs = slice(c0, c0 + FF_TILE)
        if c0:
            a_ext[HALO:HALO + tb, cs] = _dot(hn_scr[...], wup_ref[:, cs])
            gate = _dot(hn_scr[...], wup_ref[:, D_FF + c0:D_FF + c0 + FF_TILE])
        else:
            gate = jnp.concatenate(gate_parts, axis=0)
        u = (cb_ref[:, cs]
             + a_ext[HALO - 2:HALO - 2 + tb, cs] * cw_ref[0:1, cs]
             + a_ext[HALO - 1:HALO - 1 + tb, cs] * cw_ref[1:2, cs]
             + a_ext[HALO:HALO + tb, cs] * cw_ref[2:3, cs])
        u_scr[:, cs] = (u * _sigmoid(u) * gate).astype(_BF16)
    for r0 in range(0, tb, tb // 4):
        rows = slice(r0, r0 + tb // 4)
        y = h_ref[0, rows, :] + _dot(u_scr[rows, :], wdn_ref[...])
        o_ref[0, rows, :] = (y * lax.rsqrt(jnp.mean(y * y, axis=-1, keepdims=True) + EPS)
                             * fw_ref[...])


def _const_spec(shape):
    zeros = (0,) * len(shape)
    return pl.BlockSpec(shape, lambda b, t: zeros, pipeline_mode=pl.Buffered(1))


def kernel(x, attn_norm_w, w_in, hgrn_lower_bounds, w_gla_gate_up, b_gla_gate, hg_norm_w,
           gla_norm_w, w_out, ffn_norm_w, w_up, conv_w, conv_b, w_down, final_norm_w):
    batch, seq, d = x.shape
    assert d == D_MODEL and seq % MIXER_BLOCK == 0 and seq % FFN_BLOCK == 0
    assert w_in.shape[0] == 1, "single-layer problem"

    assert w_in.shape[2] == _GD + GLA_GATE_RANK
    win_t = jnp.transpose(w_in[0])
    wgu = w_gla_gate_up[0]
    wgu_hi = wgu.astype(_BF16)
    wgu_lo = (wgu - wgu_hi.astype(_F32)).astype(_BF16)
    head_nw = jnp.concatenate([hg_norm_w[0], gla_norm_w[0]])[None, :]
    n2_np, mask_np = _level_matrices(CHUNK)
    n2 = jnp.asarray(n2_np, dtype=_BF16)
    masks = jnp.asarray(mask_np)

    params = pltpu.CompilerParams(dimension_semantics=("arbitrary", "arbitrary"),
                                  vmem_limit_bytes=VMEM_LIMIT)

    def act_spec(block):
        return pl.BlockSpec((1, block, D_MODEL), lambda b, t: (b, t, 0))

    steps_per_row = seq // MIXER_BLOCK
    n_steps = batch * steps_per_row
    assert D_MODEL % n_steps == 0 and D_FF % WDN_SLAB == 0 and D_FF // WDN_SLAB <= n_steps
    wup_rows = D_MODEL // n_steps
    last_wdn = D_FF // WDN_SLAB - 1

    def step(b, t):
        return b * steps_per_row + t

    h, w_up_bf16, w_down_bf16 = pl.pallas_call(
        _mixer_kernel,
        grid=(batch, steps_per_row),
        in_specs=[
            act_spec(MIXER_BLOCK),
            _const_spec((1, D_MODEL)),
            _const_spec(win_t.shape),
            _const_spec(hgrn_lower_bounds.shape),
            _const_spec((GLA_GATE_RANK, GLA_KEY_WIDTH)),
            _const_spec((GLA_GATE_RANK, GLA_KEY_WIDTH)),
            _const_spec((1, GLA_KEY_WIDTH)),
            _const_spec((1, MIX)),
            _const_spec(w_out.shape),
            _const_spec(n2.shape),
            _const_spec(masks.shape),
            pl.BlockSpec((1, wup_rows, 2 * D_FF), lambda b, t: (0, step(b, t), 0)),
            pl.BlockSpec((1, WDN_SLAB, D_MODEL),
                         lambda b, t: (0, jnp.minimum(step(b, t), last_wdn), 0)),
        ],
        out_specs=[
            act_spec(MIXER_BLOCK),
            pl.BlockSpec((wup_rows, 2 * D_FF), lambda b, t: (step(b, t), 0)),
            pl.BlockSpec((WDN_SLAB, D_MODEL), lambda b, t: (jnp.minimum(step(b, t), last_wdn), 0)),
        ],
        out_shape=[
            jax.ShapeDtypeStruct(x.shape, _F32),
            jax.ShapeDtypeStruct((D_MODEL, 2 * D_FF), _BF16),
            jax.ShapeDtypeStruct((D_FF, D_MODEL), _BF16),
        ],
        scratch_shapes=[
            pltpu.VMEM((MIXER_BLOCK, PROJ_W), _F32),
            *[pltpu.VMEM((MIXER_BLOCK // CHUNK, n2.shape[0] + CHUNK, SLAB), _F32)
              for _ in range(N_SLABS)],
            pltpu.VMEM((MIXER_BLOCK // CHUNK, CHUNK, HG_WIDTH), _F32),
            pltpu.VMEM((MIXER_BLOCK // CHUNK, CHUNK, KEYS), _BF16),
            pltpu.VMEM((MIXER_BLOCK // CHUNK, CHUNK, KEYS), _BF16),
            pltpu.VMEM((MIXER_BLOCK // CHUNK, SUBLANES, KEYS), _F32),
            pltpu.VMEM((MIXER_BLOCK, MIX), _F32),
            pltpu.VMEM((N_HEADS, CHUNK, CHUNK), _BF16),
            pltpu.VMEM((N_HEADS, HEAD, LANE), _F32),
            pltpu.VMEM((MIXER_BLOCK, MIX), _BF16),
        ],
        compiler_params=params,
        name="mixer",
    )(x, attn_norm_w, win_t, hgrn_lower_bounds, wgu_hi, wgu_lo, b_gla_gate, head_nw,
      w_out, n2, masks, w_up, w_down)

    out = pl.pallas_call(
        _ffn_kernel,
        grid=(batch, seq // FFN_BLOCK),
        in_specs=[
            act_spec(FFN_BLOCK),
            _const_spec((1, D_MODEL)),
            _const_spec((D_MODEL, 2 * D_FF)),
            _const_spec((3, D_FF)),
            _const_spec((1, D_FF)),
            _const_spec((D_FF, D_MODEL)),
            _const_spec((1, D_MODEL)),
        ],
        out_specs=act_spec(FFN_BLOCK),
        out_shape=jax.ShapeDtypeStruct(x.shape, _F32),
        scratch_shapes=[
            pltpu.VMEM((HALO + FFN_BLOCK, D_FF), _F32),
            pltpu.VMEM((FFN_BLOCK, D_FF), _BF16),
            pltpu.VMEM((FFN_BLOCK, D_MODEL), _BF16),
        ],
        compiler_params=params,
        name="ffn",
    )(h, ffn_norm_w, w_up_bf16, conv_w[0], conv_b, w_down_bf16,
      final_norm_w[None, :])
    return out
```
